```python
import math
import jax, jax.numpy as jnp
from jax import lax
import numpy as np

D_MODEL = 1024
BATCH = 4
SEQ = 4096
DEPTH = 4
DEC_BATCH = 128
DEC_SEQ = 4
PAST_LEN = 2048
PAGE_SIZE = 128

N_MIXERS = 2
N_ATTN_LAYERS = (DEPTH + 1) // 2
N_SSD_LAYERS = DEPTH // 2
DILATED_GROUPS = ((128, 1), (512, 4), (2048, 16))
ATTN_Q_HEADS = 16
ATTN_KV_HEADS = 4
HEAD_DIM = 128
ATTN_WIDTH = ATTN_Q_HEADS * HEAD_DIM
ATTN_KV_WIDTH = ATTN_KV_HEADS * HEAD_DIM
ATTN_SPLITS = [ATTN_WIDTH] * 3 + [ATTN_KV_WIDTH] * 3 + [ATTN_KV_WIDTH] * 3 + [ATTN_WIDTH]
ATTN_IN_DIM = sum(ATTN_SPLITS)
ROPE_THETA = 10000.0
SSD_D_INNER = 2 * D_MODEL
SSD_HEAD_DIM = 64
SSD_HEADS = SSD_D_INNER // SSD_HEAD_DIM
SSD_GROUPS = 4
SSD_HEADS_PER_GROUP = SSD_HEADS // SSD_GROUPS
SSD_STATE = 128
SSD_CONV = 4
SSD_CHUNK = 128
SSD_CONV_DIM = SSD_D_INNER + 2 * SSD_GROUPS * SSD_STATE
SSD_SPLITS = [SSD_D_INNER, SSD_CONV_DIM, SSD_HEADS]
SSD_IN_DIM = sum(SSD_SPLITS)
NORM_EPS = 1e-6

kernel_name = "hybrid_dilated_swa_mamba2_decoder_step"


def split_sizes(x, sizes):
    offsets = np.cumsum(sizes)[:-1].tolist()
    return jnp.split(x, offsets, axis=-1)


def rmsnorm(x, w):
    xf = x.astype(jnp.float32)
    y = xf * lax.rsqrt(jnp.mean(xf * xf, axis=-1, keepdims=True) + NORM_EPS)
    return (y * w.astype(jnp.float32)).astype(x.dtype)


def rope(x, pos):
    half = HEAD_DIM // 2
    inv = ROPE_THETA ** (-jnp.arange(half, dtype=jnp.float32) / half)
    ang = pos.astype(jnp.float32)[:, None] * inv[None, :]
    cos = jnp.cos(ang)[None, :, None, :]
    sin = jnp.sin(ang)[None, :, None, :]
    xf = x.astype(jnp.float32)
    x1, x2 = xf[..., :half], xf[..., half:]
    return jnp.concatenate([x1 * cos - x2 * sin, x2 * cos + x1 * sin], axis=-1).astype(x.dtype)


def softmax_with_lse(s):
    m = jnp.max(s, axis=-1, keepdims=True)
    p = jnp.exp(s - m)
    den = jnp.sum(p, axis=-1, keepdims=True)
    return p / den, (m + jnp.log(den))[..., 0]


def to_strided(t, d):
    n, l = t.shape[:2]
    rest = t.shape[2:]
    t = jnp.swapaxes(t.reshape((n, l // d, d) + rest), 1, 2)
    return t.reshape((n * d, l // d) + rest)


def from_strided(t, n, d):
    m = t.shape[1]
    rest = t.shape[2:]
    t = jnp.swapaxes(t.reshape((n, d, m) + rest), 1, 2)
    return t.reshape((n, m * d) + rest)


def banded_attention(q, k, v, blk):
    n, l, hq, hd = q.shape
    hkv = k.shape[2]
    rep = hq // hkv
    pad = (-l) % blk
    lp = l + pad
    nb = lp // blk
    qb = jnp.pad(q, ((0, 0), (0, pad), (0, 0), (0, 0))).reshape(n, nb, blk, hkv, rep, hd)

    def blocks(t):
        tp = jnp.pad(t, ((0, 0), (blk, pad), (0, 0), (0, 0)))
        prev = tp[:, :lp].reshape(n, nb, blk, hkv, hd)
        cur = tp[:, blk:].reshape(n, nb, blk, hkv, hd)
        return jnp.concatenate([prev, cur], axis=2)

    kb, vb = blocks(k), blocks(v)
    qi = jnp.arange(blk)[:, None]
    kj = jnp.arange(2 * blk)[None, :]
    dist = blk + qi - kj
    band = (dist >= 0) & (dist <= blk)
    kpos = jnp.arange(nb)[:, None, None] * blk - blk + kj[None]
    mask = band[None] & (kpos >= 0)
    s = jnp.einsum('nbqgrd,nbkgd->nbgrqk', qb, kb).astype(jnp.float32) * (hd ** -0.5)
    s = jnp.where(mask[None, :, None, None], s, -jnp.inf)
    p, lse = softmax_with_lse(s)
    o = jnp.einsum('nbgrqk,nbkgd->nbqgrd', p, vb.astype(jnp.float32))
    o = o.reshape(n, lp, hq, hd)[:, :l]
    lse = jnp.transpose(lse, (0, 1, 4, 2, 3)).reshape(n, lp, hq)[:, :l]
    return o, lse


def dilated_decode_attention(q, kc, vc, wb, dil, steps):
    n, t, hq, hd = q.shape
    hkv = kc.shape[2]
    rep = hq // hkv
    idx = wb + jnp.arange(t)[:, None] - dil * jnp.arange(steps + 1)[None, :]
    valid = idx >= 0
    idx = jnp.maximum(idx, 0)
    kg = kc[:, idx]
    vg = vc[:, idx]
    qg = q.reshape(n, t, hkv, rep, hd)
    s = jnp.einsum('ntgrd,ntjgd->ntgrj', qg, kg).astype(jnp.float32) * (hd ** -0.5)
    s = jnp.where(valid[None, :, None, None, :], s, -jnp.inf)
    p, lse = softmax_with_lse(s)
    o = jnp.einsum('ntgrj,ntjgd->ntgrd', p, vg.astype(jnp.float32))
    return o.reshape(n, t, hq, hd), lse.reshape(n, t, hq)


def merge_groups(outs, lses):
    w = jax.nn.softmax(jnp.stack(lses, axis=0), axis=0)
    return jnp.sum(w[..., None] * jnp.stack(outs, axis=0), axis=0)


def attn_project(u, w_in, pos):
    n, l, _ = u.shape
    parts = split_sizes(u @ w_in, ATTN_SPLITS)
    qs = [rope(p.reshape(n, l, ATTN_Q_HEADS, HEAD_DIM), pos) for p in parts[0:3]]
    ks = [rope(p.reshape(n, l, ATTN_KV_HEADS, HEAD_DIM), pos) for p in parts[3:6]]
    vs = [p.reshape(n, l, ATTN_KV_HEADS, HEAD_DIM) for p in parts[6:9]]
    return qs, ks, vs, parts[9]


def attn_prompt(u, w_in, w_out):
    n, l, _ = u.shape
    qs, ks, vs, gate = attn_project(u, w_in, jnp.arange(l))
    outs, lses, bufs = [], [], []
    for g, (win, dil) in enumerate(DILATED_GROUPS):
        o, lse = banded_attention(to_strided(qs[g], dil), to_strided(ks[g], dil),
                                  to_strided(vs[g], dil), win // dil)
        outs.append(from_strided(o, n, dil))
        lses.append(from_strided(lse, n, dil))
        keep = min(win, l)
        bufs.append(jnp.stack([ks[g][:, l - keep:], vs[g][:, l - keep:]], axis=2))
    y = merge_groups(outs, lses).reshape(n, l, ATTN_WIDTH).astype(u.dtype)
    return (y * jax.nn.silu(gate)) @ w_out, bufs


def attn_sample(u, w_in, w_out, bufs):
    n, t, _ = u.shape
    qs, ks, vs, gate = attn_project(u, w_in, PAST_LEN + jnp.arange(t))
    outs, lses, new_bufs = [], [], []
    for g, (win, dil) in enumerate(DILATED_GROUPS):
        buf = bufs[g].astype(u.dtype)
        wb = buf.shape[1]
        kc = jnp.concatenate([buf[:, :, 0], ks[g]], axis=1)
        vc = jnp.concatenate([buf[:, :, 1], vs[g]], axis=1)
        o, lse = dilated_decode_attention(qs[g], kc, vc, wb, dil, win // dil)
        outs.append(o)
        lses.append(lse)
        keep = min(win, PAST_LEN + t)
        new_bufs.append(jnp.stack([kc[:, -keep:], vc[:, -keep:]], axis=2))
    y = merge_groups(outs, lses).reshape(n, t, ATTN_WIDTH).astype(u.dtype)
    return (y * jax.nn.silu(gate)) @ w_out, new_bufs


def causal_conv(x, prev, w, b):
    xp = jnp.concatenate([prev.astype(x.dtype), x], axis=1)
    c = x.shape[-1]
    y = lax.conv_general_dilated(xp, w.astype(x.dtype)[:, None, :], window_strides=(1,),
                                 padding='VALID', dimension_numbers=('NWC', 'WIO', 'NWC'),
                                 feature_group_count=c)
    return y + b.astype(x.dtype), xp[:, -(SSD_CONV - 1):]


def ssd_scan(x, dt, a, bm, cm, h0):
    n, l, g, e, p = x.shape
    s_dim = bm.shape[-1]
    q = min(SSD_CHUNK, l)
    pad = (-l) % q
    lp = l + pad
    c = lp // q
    f32 = jnp.float32

    def chunks(t):
        t = jnp.pad(t.astype(f32), ((0, 0), (0, pad)) + ((0, 0),) * (t.ndim - 2))
        return t.reshape((n, c, q) + t.shape[2:])

    xc, dtc, bc, cc = chunks(x), chunks(dt), chunks(bm), chunks(cm)
    dt_t = jnp.transpose(dtc, (0, 1, 3, 4, 2))
    acs = jnp.cumsum(dt_t * a.astype(f32)[None, None, :, :, None], axis=-1)
    tri = jnp.tril(jnp.ones((q, q), dtype=bool))
    seg = acs[..., :, None] - acs[..., None, :]
    decay = jnp.exp(jnp.where(tri, seg, -jnp.inf))
    cb = jnp.einsum('ncigs,ncjgs->ncgij', cc, bc)
    wts = cb[:, :, :, None] * decay * dt_t[..., None, :]
    y_diag = jnp.einsum('ncgeij,ncjgep->ncigep', wts, xc)
    decay_end = jnp.exp(acs[..., -1:] - acs) * dt_t
    states = jnp.einsum('ncjgs,ncgej,ncjgep->ncgeps', bc, decay_end, xc)
    chunk_decay = jnp.exp(acs[..., -1])

    def step(h, inp):
        dec, st = inp
        return dec[..., None, None] * h + st, h

    h_final, h_starts = lax.scan(step, h0.astype(f32),
                                 (jnp.moveaxis(chunk_decay, 1, 0), jnp.moveaxis(states, 1, 0)))
    h_starts = jnp.moveaxis(h_starts, 0, 1)
    y_off = jnp.einsum('ncigs,ncgeps,ncgei->ncigep', cc, h_starts, jnp.exp(acs))
    y = (y_diag + y_off).reshape(n, lp, g, e, p)[:, :l]
    return y, h_final


def ssd_mixer(u, w_in, conv_w, conv_b, dt_bias, a_log, d_skip, norm_w, w_out, h0, conv0):
    n, l, _ = u.shape
    z, xbc, dt_raw = split_sizes(u @ w_in, SSD_SPLITS)
    xbc, conv_new = causal_conv(xbc, conv0, conv_w, conv_b)
    xbc = jax.nn.silu(xbc)
    xs, bm, cm = split_sizes(xbc, [SSD_D_INNER, SSD_GROUPS * SSD_STATE, SSD_GROUPS * SSD_STATE])
    xs = xs.reshape(n, l, SSD_GROUPS, SSD_HEADS_PER_GROUP, SSD_HEAD_DIM)
    bm = bm.reshape(n, l, SSD_GROUPS, SSD_STATE)
    cm = cm.reshape(n, l, SSD_GROUPS, SSD_STATE)
    dt = jax.nn.softplus(dt_raw.astype(jnp.float32) + dt_bias.astype(jnp.float32))
    dt = dt.reshape(n, l, SSD_GROUPS, SSD_HEADS_PER_GROUP)
    a = -jnp.exp(a_log.astype(jnp.float32)).reshape(SSD_GROUPS, SSD_HEADS_PER_GROUP)
    h0g = h0.reshape(n, SSD_GROUPS, SSD_HEADS_PER_GROUP, SSD_HEAD_DIM, SSD_STATE)
    y, h_new = ssd_scan(xs, dt, a, bm, cm, h0g)
    y = y + d_skip.astype(jnp.float32).reshape(SSD_GROUPS, SSD_HEADS_PER_GROUP, 1) * xs.astype(jnp.float32)
    y = y.reshape(n, l, SSD_D_INNER) * jax.nn.silu(z.astype(jnp.float32))
    yg = y.reshape(n, l, SSD_GROUPS, SSD_D_INNER // SSD_GROUPS)
    yg = yg * lax.rsqrt(jnp.mean(yg * yg, axis=-1, keepdims=True) + NORM_EPS)
    y = (yg.reshape(n, l, SSD_D_INNER) * norm_w.astype(jnp.float32)).astype(u.dtype)
    h_new = h_new.reshape(n, SSD_HEADS, SSD_HEAD_DIM, SSD_STATE).astype(h0.dtype)
    return y @ w_out, h_new, conv_new


def setup_inputs(seed: int = 0) -> dict:
    key = jax.random.key(seed)
    ks = jax.random.split(key, 20)
    f32 = jnp.float32

    def nrm(k, shape, scale):
        return scale * jax.random.normal(k, shape, f32)

    def kv_shape(win, batch, past):
        return (N_ATTN_LAYERS, batch, min(win, past), 2, ATTN_KV_HEADS, HEAD_DIM)

    dt0 = jnp.exp(jax.random.uniform(ks[15], (N_SSD_LAYERS, SSD_HEADS), f32,
                                     math.log(1e-3), math.log(1e-1)))
    return {
        "x_prompt": nrm(ks[0], (BATCH, SEQ, D_MODEL), 1.0),
        "x_sample": nrm(ks[1], (DEC_BATCH, DEC_SEQ, D_MODEL), 1.0),
        "cache_kv_w128": nrm(ks[2], kv_shape(DILATED_GROUPS[0][0], DEC_BATCH, PAST_LEN), 1.0),
        "cache_kv_w512": nrm(ks[3], kv_shape(DILATED_GROUPS[1][0], DEC_BATCH, PAST_LEN), 1.0),
        "cache_kv_w2048": nrm(ks[4], kv_shape(DILATED_GROUPS[2][0], DEC_BATCH, PAST_LEN), 1.0),
        "state_ssm": nrm(ks[5], (N_SSD_LAYERS, DEC_BATCH, SSD_HEADS, SSD_HEAD_DIM, SSD_STATE), 0.1),
        "state_conv": nrm(ks[6], (N_SSD_LAYERS, DEC_BATCH, SSD_CONV - 1, SSD_CONV_DIM), 1.0),
        "norm_pre": 1.0 + nrm(ks[7], (DEPTH, D_MODEL), 0.05),
        "norm_post": 1.0 + nrm(ks[8], (DEPTH, D_MODEL), 0.05),
        "attn_w_in": nrm(ks[9], (N_ATTN_LAYERS, D_MODEL, ATTN_IN_DIM), D_MODEL ** -0.5),
        "attn_w_out": nrm(ks[10], (N_ATTN_LAYERS, ATTN_WIDTH, D_MODEL), ATTN_WIDTH ** -0.5),
        "ssd_w_in": nrm(ks[11], (N_SSD_LAYERS, D_MODEL, SSD_IN_DIM), D_MODEL ** -0.5),
        "ssd_conv_w": nrm(ks[12], (N_SSD_LAYERS, SSD_CONV, SSD_CONV_DIM), SSD_CONV ** -0.5),
        "ssd_conv_b": nrm(ks[13], (N_SSD_LAYERS, SSD_CONV_DIM), 0.02),
        "ssd_dt_bias": dt0 + jnp.log(-jnp.expm1(-dt0)),
        "ssd_a_log": jnp.log(jax.random.uniform(ks[14], (N_SSD_LAYERS, SSD_HEADS), f32, 1.0, 16.0)),
        "ssd_d": 1.0 + nrm(ks[16], (N_SSD_LAYERS, SSD_HEADS), 0.1),
        "ssd_norm_w": 1.0 + nrm(ks[17], (N_SSD_LAYERS, SSD_D_INNER), 0.05),
        "ssd_w_out": nrm(ks[18], (N_SSD_LAYERS, SSD_D_INNER, D_MODEL), SSD_D_INNER ** -0.5),
    }


def reference(x_prompt, x_sample, cache_kv_w128, cache_kv_w512, cache_kv_w2048, state_ssm, state_conv,
              norm_pre, norm_post, attn_w_in, attn_w_out, ssd_w_in, ssd_conv_w, ssd_conv_b,
              ssd_dt_bias, ssd_a_log, ssd_d, ssd_norm_w, ssd_w_out):
    hp, hs = x_prompt, x_sample
    p_kv, s_kv = [[], [], []], [[], [], []]
    p_ssm, p_conv, s_ssm, s_conv = [], [], [], []
    for i in range(DEPTH):
        j = i // N_MIXERS
        up = rmsnorm(hp, norm_pre[i])
        us = rmsnorm(hs, norm_pre[i])
        if i % N_MIXERS == 0:
            yp, bufs_p = attn_prompt(up, attn_w_in[j], attn_w_out[j])
            ys, bufs_s = attn_sample(us, attn_w_in[j], attn_w_out[j],
                                     [cache_kv_w128[j], cache_kv_w512[j], cache_kv_w2048[j]])
            for g in range(len(DILATED_GROUPS)):
                p_kv[g].append(bufs_p[g])
                s_kv[g].append(bufs_s[g])
        else:
            n_p = up.shape[0]
            h0p = jnp.zeros((n_p, SSD_HEADS, SSD_HEAD_DIM, SSD_STATE), up.dtype)
            c0p = jnp.zeros((n_p, SSD_CONV - 1, SSD_CONV_DIM), up.dtype)
            yp, hp_new, cp_new = ssd_mixer(up, ssd_w_in[j], ssd_conv_w[j], ssd_conv_b[j], ssd_dt_bias[j],
                                           ssd_a_log[j], ssd_d[j], ssd_norm_w[j], ssd_w_out[j], h0p, c0p)
            ys, hs_new, cs_new = ssd_mixer(us, ssd_w_in[j], ssd_conv_w[j], ssd_conv_b[j], ssd_dt_bias[j],
                                           ssd_a_log[j], ssd_d[j], ssd_norm_w[j], ssd_w_out[j],
                                           state_ssm[j], state_conv[j])
            p_ssm.append(hp_new)
            p_conv.append(cp_new)
            s_ssm.append(hs_new)
            s_conv.append(cs_new)
        hp = hp + rmsnorm(yp, norm_post[i])
        hs = hs + rmsnorm(ys, norm_post[i])
    p_kv_w128, p_kv_w512, p_kv_w2048 = jnp.stack(p_kv[0]), jnp.stack(p_kv[1]), jnp.stack(p_kv[2])
    s_kv_w128, s_kv_w512, s_kv_w2048 = jnp.stack(s_kv[0]), jnp.stack(s_kv[1]), jnp.stack(s_kv[2])
    p_ssm_a, p_conv_a = jnp.stack(p_ssm), jnp.stack(p_conv)
    s_ssm_a, s_conv_a = jnp.stack(s_ssm), jnp.stack(s_conv)
    return (hp, hs, p_kv_w128, p_kv_w512, p_kv_w2048, p_ssm_a, p_conv_a,
            s_kv_w128, s_kv_w512, s_kv_w2048, s_ssm_a, s_conv_a)
```

```python
import functools
import math

import jax
import jax.numpy as jnp
from jax import lax
from jax.experimental import pallas as pl
from jax.experimental.pallas import tpu as pltpu

F32 = jnp.float32
BF16 = jnp.bfloat16

D_MODEL = 1024
HEAD_DIM = 128
HALF_HEAD = HEAD_DIM // 2
Q_HEADS = 16
KV_HEADS = 4
REP = Q_HEADS // KV_HEADS
ATTN_WIDTH = Q_HEADS * HEAD_DIM
KV_WIDTH = KV_HEADS * HEAD_DIM
DILATED_GROUPS = ((128, 1), (512, 4), (2048, 16))
N_GROUPS = len(DILATED_GROUPS)
BAND = 128
ROPE_THETA = 10000.0
PAST_LEN = 2048
SSD_D_INNER = 2048
SSD_HEAD_DIM = 64
SSD_HEADS = 32
SSD_GROUPS = 4
SSD_GROUP_WIDTH = SSD_D_INNER // SSD_GROUPS
SSD_STATE = 128
SSD_CONV = 4
SSD_CHUNK = 128
SSD_BC_WIDTH = SSD_GROUPS * SSD_STATE
SSD_CONV_DIM = SSD_D_INNER + 2 * SSD_BC_WIDTH
NORM_EPS = 1e-6
LANES = 128
SUBLANES = 8
VMEM_LIMIT_BYTES = 48 * 1024 * 1024
NEG_INF = float("-inf")


def _params(*semantics):
    return pltpu.CompilerParams(dimension_semantics=semantics, vmem_limit_bytes=VMEM_LIMIT_BYTES)


def _nt_dot(a, b):
    return lax.dot_general(a, b, (((1,), (1,)), ((), ())), preferred_element_type=F32)


def _split3(a):
    hi = a.astype(BF16)
    r1 = a - hi.astype(F32)
    mid = r1.astype(BF16)
    lo = (r1 - mid.astype(F32)).astype(BF16)
    return hi, mid, lo


def _sel_dot_right(a, sel):
    hi, mid, lo = _split3(a)
    return (jnp.dot(hi, sel, preferred_element_type=F32) + jnp.dot(mid, sel, preferred_element_type=F32)
            + jnp.dot(lo, sel, preferred_element_type=F32))


def _sel_dot_left(sel, a):
    hi, mid, lo = _split3(a)
    return (jnp.dot(sel, hi, preferred_element_type=F32) + jnp.dot(sel, mid, preferred_element_type=F32)
            + jnp.dot(sel, lo, preferred_element_type=F32))


def _rms(x, w):
    return x * lax.rsqrt(jnp.mean(x * x, axis=-1, keepdims=True) + NORM_EPS) * w


def _rmsnorm_kernel(x_ref, w_ref, o_ref):
    o_ref[...] = _rms(x_ref[...], w_ref[...]).astype(o_ref.dtype)


def _rmsnorm(x, w, tm):
    t, d = x.shape
    return pl.pallas_call(
        _rmsnorm_kernel,
        grid=(t // tm,),
        in_specs=[pl.BlockSpec((tm, d), lambda i: (i, 0)), pl.BlockSpec((1, d), lambda i: (0, 0))],
        out_specs=pl.BlockSpec((tm, d), lambda i: (i, 0)),
        out_shape=jax.ShapeDtypeStruct((t, d), BF16),
        compiler_params=_params("parallel"),
        name="rmsnorm",
    )(x, w.reshape(1, d))


def _proj_kernel(u_ref, w_ref, o_ref):
    o_ref[...] = jnp.dot(u_ref[...], w_ref[...], preferred_element_type=F32).astype(o_ref.dtype)


def _proj_rope_kernel(u_ref, w_ref, cos_ref, sin_ref, o_ref):
    acc = jnp.dot(u_ref[...], w_ref[...], preferred_element_type=F32)
    cos = cos_ref[...]
    sin = sin_ref[...]
    for c in range(acc.shape[1] // HEAD_DIM):
        x = acc[:, c * HEAD_DIM:(c + 1) * HEAD_DIM]
        o_ref[:, c * HEAD_DIM:(c + 1) * HEAD_DIM] = (
            x * cos + pltpu.roll(x, HALF_HEAD, 1) * sin).astype(o_ref.dtype)


def _proj(u, w, out_dtype, tm, tn, rope=None):
    t, k = u.shape
    n = w.shape[1]
    tn = min(tn, n)
    grid = (t // tm, n // tn)
    in_specs = [pl.BlockSpec((tm, k), lambda i, j: (i, 0)), pl.BlockSpec((k, tn), lambda i, j: (0, j))]
    args = [u, w]
    kern = _proj_kernel
    if rope is not None:
        cos, sin = rope
        nblk = cos.shape[0] // tm
        in_specs += [pl.BlockSpec((tm, HEAD_DIM), lambda i, j: (i % nblk, 0)),
                     pl.BlockSpec((tm, HEAD_DIM), lambda i, j: (i % nblk, 0))]
        args += [cos, sin]
        kern = _proj_rope_kernel
    return pl.pallas_call(
        kern,
        grid=grid,
        in_specs=in_specs,
        out_specs=pl.BlockSpec((tm, tn), lambda i, j: (i, j)),
        out_shape=jax.ShapeDtypeStruct((t, n), out_dtype),
        compiler_params=_params("parallel", "parallel"),
        name="proj_rope" if rope is not None else "proj",
    )(*args)


def _band_attn_kernel(q_ref, kp_ref, kc_ref, vp_ref, vc_ref, o_ref, lse_ref):
    mb = pl.program_id(1)
    rows = REP * BAND
    qi = lax.broadcasted_iota(jnp.int32, (rows, 2 * BAND), 0) % BAND
    kj = lax.broadcasted_iota(jnp.int32, (rows, 2 * BAND), 1)
    dist = BAND + qi - kj
    has_prev = mb > 0
    valid = (dist >= 0) & (dist <= BAND) & ((kj >= BAND) | has_prev)
    lane = lax.broadcasted_iota(jnp.int32, (BAND, LANES), 1)
    lse_tile = jnp.zeros((BAND, LANES), F32)
    scale = HEAD_DIM ** -0.5
    for g in range(KV_HEADS):
        cs = slice(g * HEAD_DIM, (g + 1) * HEAD_DIM)
        k2 = jnp.concatenate([kp_ref[:, cs], kc_ref[:, cs]], axis=0).astype(BF16)
        v2 = jnp.concatenate([vp_ref[:, cs], vc_ref[:, cs]], axis=0).astype(BF16)
        q4 = jnp.concatenate(
            [q_ref[:, (g * REP + r) * HEAD_DIM:(g * REP + r + 1) * HEAD_DIM] for r in range(REP)], axis=0)
        s = _nt_dot(q4, k2) * scale
        s = jnp.where(valid, s, NEG_INF)
        m = jnp.max(s, axis=-1, keepdims=True)
        p = jnp.exp(s - m)
        den = jnp.sum(p, axis=-1, keepdims=True)
        o = jnp.dot(p.astype(BF16), v2, preferred_element_type=F32) / den
        lse = m + jnp.log(den)
        for r in range(REP):
            h = g * REP + r
            o_ref[:, h * HEAD_DIM:(h + 1) * HEAD_DIM] = o[r * BAND:(r + 1) * BAND].astype(o_ref.dtype)
            lse_tile = jnp.where(lane == h, lse[r * BAND:(r + 1) * BAND], lse_tile)
    lse_ref[...] = lse_tile


def _band_attn(q, k, v, group, dil, n_batch, seq_len):
    t = q.shape[0]
    rows = t // dil
    blocks_per_seq = seq_len // dil // BAND
    qv = q.reshape(rows, dil * N_GROUPS * ATTN_WIDTH)
    kv = k.reshape(rows, dil * N_GROUPS * KV_WIDTH)
    vv = v.reshape(rows, dil * N_GROUPS * KV_WIDTH)

    def row_blk(s, mb):
        return (s // dil) * blocks_per_seq + mb

    def cur_map(s, mb):
        return (row_blk(s, mb), (s % dil) * N_GROUPS + group)

    def prev_map(s, mb):
        return (row_blk(s, jnp.maximum(mb - 1, 0)), (s % dil) * N_GROUPS + group)

    def out_map(s, mb):
        return (row_blk(s, mb), s % dil)

    o, lse = pl.pallas_call(
        _band_attn_kernel,
        grid=(n_batch * dil, blocks_per_seq),
        in_specs=[pl.BlockSpec((BAND, ATTN_WIDTH), cur_map),
                  pl.BlockSpec((BAND, KV_WIDTH), prev_map), pl.BlockSpec((BAND, KV_WIDTH), cur_map),
                  pl.BlockSpec((BAND, KV_WIDTH), prev_map), pl.BlockSpec((BAND, KV_WIDTH), cur_map)],
        out_specs=[pl.BlockSpec((BAND, ATTN_WIDTH), out_map), pl.BlockSpec((BAND, LANES), out_map)],
        out_shape=[jax.ShapeDtypeStruct((rows, dil * ATTN_WIDTH), BF16),
                   jax.ShapeDtypeStruct((rows, dil * LANES), F32)],
        compiler_params=_params("parallel", "arbitrary"),
        name="band_attn_d%d" % dil,
    )(qv, kv, kv, vv, vv)
    return o.reshape(t, ATTN_WIDTH), lse.reshape(t, LANES)


def _merge_gate_kernel(o0_ref, o1_ref, o2_ref, l0_ref, l1_ref, l2_ref, gate_ref, y_ref):
    l0, l1, l2 = l0_ref[...], l1_ref[...], l2_ref[...]
    m = jnp.maximum(jnp.maximum(l0, l1), l2)
    e0, e1, e2 = jnp.exp(l0 - m), jnp.exp(l1 - m), jnp.exp(l2 - m)
    den = e0 + e1 + e2
    w0, w1, w2 = e0 / den, e1 / den, e2 / den
    for h in range(Q_HEADS):
        cs = slice(h * HEAD_DIM, (h + 1) * HEAD_DIM)
        y = (w0[:, h:h + 1] * o0_ref[:, cs].astype(F32) + w1[:, h:h + 1] * o1_ref[:, cs].astype(F32)
             + w2[:, h:h + 1] * o2_ref[:, cs].astype(F32))
        gate = gate_ref[:, cs]
        y_ref[:, cs] = (y * (gate * jax.nn.sigmoid(gate))).astype(y_ref.dtype)


def _merge_gate(os_, lses, gate, tm):
    t = gate.shape[0]
    wide = pl.BlockSpec((tm, ATTN_WIDTH), lambda i: (i, 0))
    narrow = pl.BlockSpec((tm, LANES), lambda i: (i, 0))
    return pl.pallas_call(
        _merge_gate_kernel,
        grid=(t // tm,),
        in_specs=[wide, wide, wide, narrow, narrow, narrow, wide],
        out_specs=wide,
        out_shape=jax.ShapeDtypeStruct((t, ATTN_WIDTH), BF16),
        compiler_params=_params("parallel"),
        name="merge_gate",
    )(*os_, *lses, gate)


def _out_kernel(y_ref, w_ref, h_ref, pw_ref, nw_ref, hn_ref, un_ref):
    y2 = jnp.dot(y_ref[...], w_ref[...], preferred_element_type=F32)
    hn = h_ref[...] + _rms(y2, pw_ref[...])
    hn_ref[...] = hn
    un_ref[...] = _rms(hn, nw_ref[...]).astype(un_ref.dtype)


def _out_last_kernel(y_ref, w_ref, h_ref, pw_ref, hn_ref):
    y2 = jnp.dot(y_ref[...], w_ref[...], preferred_element_type=F32)
    hn_ref[...] = h_ref[...] + _rms(y2, pw_ref[...])


def _out_proj(y, w, h, post_w, next_w, tm):
    t, k = y.shape
    d = w.shape[1]
    row = lambda i: (i, 0)
    fixed = lambda i: (0, 0)
    in_specs = [pl.BlockSpec((tm, k), row), pl.BlockSpec((k, d), fixed), pl.BlockSpec((tm, d), row),
                pl.BlockSpec((1, d), fixed)]
    args = [y, w, h, post_w.reshape(1, d)]
    if next_w is None:
        return pl.pallas_call(
            _out_last_kernel, grid=(t // tm,), in_specs=in_specs,
            out_specs=pl.BlockSpec((tm, d), row), out_shape=jax.ShapeDtypeStruct((t, d), F32),
            compiler_params=_params("parallel"), name="out_proj_last")(*args), None
    in_specs.append(pl.BlockSpec((1, d), fixed))
    args.append(next_w.reshape(1, d))
    return pl.pallas_call(
        _out_kernel, grid=(t // tm,), in_specs=in_specs,
        out_specs=[pl.BlockSpec((tm, d), row), pl.BlockSpec((tm, d), row)],
        out_shape=[jax.ShapeDtypeStruct((t, d), F32), jax.ShapeDtypeStruct((t, d), BF16)],
        compiler_params=_params("parallel"), name="out_proj")(*args)


def _decode_attn_kernel(q_ref, k_ref, v_ref, c0_ref, c1_ref, c2_ref, o_ref, lse_ref, *, n_tok):
    rows = n_tok * REP
    scale = HEAD_DIM ** -0.5
    row_t = lax.broadcasted_iota(jnp.int32, (rows, BAND), 0) // REP
    col = lax.broadcasted_iota(jnp.int32, (rows, BAND), 1)
    row_t1 = row_t[:, :1]
    lane = lax.broadcasted_iota(jnp.int32, (rows, LANES), 1)
    lse_tile = jnp.zeros((rows, LANES), F32)
    crefs = (c0_ref, c1_ref, c2_ref)
    kv_row = 2 * KV_WIDTH
    for g in range(N_GROUPS):
        cref = crefs[g]
        for kh in range(KV_HEADS):
            q16 = q_ref[0, g * KV_HEADS + kh]
            hs = slice(g * KV_WIDTH + kh * HEAD_DIM, g * KV_WIDTH + (kh + 1) * HEAD_DIM)
            knew = k_ref[0][:, hs]
            vnew = v_ref[0][:, hs]
            s_tiles, v_tiles = [], []
            if g == 0:
                kc = cref[0, :, kh * HEAD_DIM:(kh + 1) * HEAD_DIM].astype(BF16)
                v_tiles.append(cref[0, :, KV_WIDTH + kh * HEAD_DIM:KV_WIDTH + (kh + 1) * HEAD_DIM].astype(BF16))
                s_tiles.append(jnp.where(col >= row_t, _nt_dot(q16, kc) * scale, NEG_INF))
            else:
                for t in range(n_tok):
                    base = t * kv_row + kh * HEAD_DIM
                    kc = cref[0, :, base:base + HEAD_DIM].astype(BF16)
                    v_tiles.append(cref[0, :, base + KV_WIDTH:base + KV_WIDTH + HEAD_DIM].astype(BF16))
                    s_tiles.append(jnp.where(row_t == t, _nt_dot(q16, kc) * scale, NEG_INF))
            qf = q16.astype(F32)
            s_new = []
            for t in range(n_tok):
                sn = jnp.sum(qf * knew[t:t + 1, :], axis=-1, keepdims=True) * scale
                ok = (row_t1 >= t) if g == 0 else (row_t1 == t)
                s_new.append(jnp.where(ok, sn, NEG_INF))
            m = s_new[0]
            for sn in s_new[1:]:
                m = jnp.maximum(m, sn)
            for st in s_tiles:
                m = jnp.maximum(m, jnp.max(st, axis=-1, keepdims=True))
            den = jnp.zeros((rows, 1), F32)
            acc = jnp.zeros((rows, HEAD_DIM), F32)
            for st, vt in zip(s_tiles, v_tiles):
                p = jnp.exp(st - m)
                den = den + jnp.sum(p, axis=-1, keepdims=True)
                acc = acc + jnp.dot(p.astype(BF16), vt, preferred_element_type=F32)
            for t in range(n_tok):
                pn = jnp.exp(s_new[t] - m)
                den = den + pn
                acc = acc + pn * vnew[t:t + 1, :]
            o_ref[0, g * KV_HEADS + kh] = (acc / den).astype(o_ref.dtype)
            lse_tile = jnp.where(lane == g * KV_HEADS + kh, m + jnp.log(den), lse_tile)
    lse_ref[0] = lse_tile


def _decode_attn(q, k, v, caches, layer, n_seq, n_tok):
    rows = n_tok * REP
    qh = q.reshape(n_seq, n_tok, N_GROUPS, KV_HEADS, REP, HEAD_DIM)
    qh = jnp.transpose(qh, (0, 2, 3, 1, 4, 5)).reshape(n_seq, N_GROUPS * KV_HEADS, rows, HEAD_DIM)
    k3 = k.reshape(n_seq, n_tok, N_GROUPS * KV_WIDTH)
    v3 = v.reshape(n_seq, n_tok, N_GROUPS * KV_WIDTH)
    kv_row = 2 * KV_WIDTH
    cviews = [c.reshape(c.shape[0] * n_seq, BAND, (c.shape[2] // BAND) * kv_row) for c in caches]
    cwidth = [kv_row, n_tok * kv_row, n_tok * kv_row]
    for g, (win, dil) in enumerate(DILATED_GROUPS):
        assert caches[g].shape[2] == win and win // dil == BAND and (g == 0 or dil >= n_tok)
    cmap = lambda b: (layer * n_seq + b, 0, 0)
    bmap3 = lambda b: (b, 0, 0)
    bmap4 = lambda b: (b, 0, 0, 0)
    o, lse = pl.pallas_call(
        functools.partial(_decode_attn_kernel, n_tok=n_tok),
        grid=(n_seq,),
        in_specs=[pl.BlockSpec((1, N_GROUPS * KV_HEADS, rows, HEAD_DIM), bmap4),
                  pl.BlockSpec((1, n_tok, N_GROUPS * KV_WIDTH), bmap3),
                  pl.BlockSpec((1, n_tok, N_GROUPS * KV_WIDTH), bmap3)]
        + [pl.BlockSpec((1, BAND, cwidth[g]), cmap) for g in range(N_GROUPS)],
        out_specs=[pl.BlockSpec((1, N_GROUPS * KV_HEADS, rows, HEAD_DIM), bmap4),
                   pl.BlockSpec((1, rows, LANES), bmap3)],
        out_shape=[jax.ShapeDtypeStruct((n_seq, N_GROUPS * KV_HEADS, rows, HEAD_DIM), BF16),
                   jax.ShapeDtypeStruct((n_seq, rows, LANES), F32)],
        compiler_params=_params("parallel"),
        name="decode_attn",
    )(qh, k3, v3, *cviews)
    o = o.reshape(n_seq, N_GROUPS, KV_HEADS, n_tok, REP, HEAD_DIM)
    o = jnp.transpose(o, (1, 0, 3, 2, 4, 5)).reshape(N_GROUPS, n_seq * n_tok, ATTN_WIDTH)
    lse = lse[:, :, :N_GROUPS * KV_HEADS].reshape(n_seq, n_tok, REP, N_GROUPS, KV_HEADS)
    lse = jnp.transpose(lse, (3, 0, 1, 4, 2)).reshape(N_GROUPS, n_seq * n_tok, Q_HEADS)
    lse = jnp.pad(lse, ((0, 0), (0, 0), (0, LANES - Q_HEADS)))
    return [o[g] for g in range(N_GROUPS)], [lse[g] for g in range(N_GROUPS)]


def _ssd_kernel(xbc_ref, z_ref, dt_ref, h0_ref, tail_ref, cw_ref, cb_ref, dtb_ref, alog_ref, dsk_ref,
                nw_ref, tri_ref, ex_ref, ext_ref, y_ref, hout_ref, state_ref, xp_ref, yacc_ref, *, valid_len):
    c = pl.program_id(1)
    q = SSD_CHUNK
    pad = SUBLANES

    @pl.when(c == 0)
    def _():
        state_ref[...] = h0_ref[0]
        xp_ref[0:pad, :] = tail_ref[0]

    xp_ref[pad:pad + q, :] = xbc_ref[...]
    conv = cb_ref[...]
    for k in range(SSD_CONV):
        off = pad - (SSD_CONV - 1) + k
        conv = conv + cw_ref[k:k + 1, :] * xp_ref[off:off + q, :]
    xp_ref[0:pad, :] = xbc_ref[q - pad:q, :]
    xc = conv * jax.nn.sigmoid(conv)
    x = xc[:, :SSD_D_INNER]
    bm = xc[:, SSD_D_INNER:SSD_D_INNER + SSD_BC_WIDTH].astype(BF16)
    cm = xc[:, SSD_D_INNER + SSD_BC_WIDTH:].astype(BF16)

    dt = jax.nn.softplus(dt_ref[...] + dtb_ref[...])
    if valid_len < q:
        trow = lax.broadcasted_iota(jnp.int32, (q, LANES), 0)
        dt = jnp.where(trow < valid_len, dt, 0.0)
    a = -jnp.exp(alog_ref[...])
    tri = tri_ref[...]
    acs = _sel_dot_left(tri, dt * a)
    acs_t = acs.T
    dt_t = dt.T
    acs_last = acs[q - 1:q, :]
    ex = ex_ref[...]
    e_in = _sel_dot_right(jnp.exp(acs), ex)
    e_end = _sel_dot_right(jnp.exp(acs_last - acs) * dt, ex)
    xs = x * e_end
    cd_col = jnp.exp(acs_t[:, q - 1:q])
    cd = _sel_dot_left(ext_ref[...], jnp.broadcast_to(cd_col, (LANES, SSD_STATE)))
    tmask = lax.broadcasted_iota(jnp.int32, (q, q), 0) >= lax.broadcasted_iota(jnp.int32, (q, q), 1)
    lane = lax.broadcasted_iota(jnp.int32, (q, LANES), 1)
    for g in range(SSD_GROUPS):
        gs = slice(g * SSD_GROUP_WIDTH, (g + 1) * SSD_GROUP_WIDTH)
        bg = bm[:, g * SSD_STATE:(g + 1) * SSD_STATE]
        cg = cm[:, g * SSD_STATE:(g + 1) * SSD_STATE]
        h_g = state_ref[gs, :]
        y_off = _nt_dot(cg, h_g.astype(BF16)) * e_in[:, gs]
        new_states = jnp.dot(xs[:, gs].T.astype(BF16), bg, preferred_element_type=F32)
        state_ref[gs, :] = cd[gs, :] * h_g + new_states
        cb = _nt_dot(cg, bg)
        heads_per_group = SSD_HEADS // SSD_GROUPS
        for pr in range(heads_per_group // 2):
            e0 = g * heads_per_group + 2 * pr
            wts = []
            for e in (e0, e0 + 1):
                seg = acs[:, e:e + 1] - acs_t[e:e + 1, :]
                decay = jnp.exp(jnp.where(tmask, seg, NEG_INF))
                wts.append((cb * decay * dt_t[e:e + 1, :]).astype(BF16))
            x2 = x[:, e0 * SSD_HEAD_DIM:(e0 + 2) * SSD_HEAD_DIM]
            rhs = jnp.concatenate([jnp.where(lane < SSD_HEAD_DIM, x2, 0.0),
                                   jnp.where(lane >= SSD_HEAD_DIM, x2, 0.0)], axis=0).astype(BF16)
            y_diag = jnp.dot(jnp.concatenate(wts, axis=1), rhs, preferred_element_type=F32)
            ls = slice(e0 * SSD_HEAD_DIM, (e0 + 2) * SSD_HEAD_DIM)
            yacc_ref[:, ls] = y_diag + y_off[:, (2 * pr) * SSD_HEAD_DIM:(2 * pr + 2) * SSD_HEAD_DIM]

    z = z_ref[...]
    y = (yacc_ref[...] + dsk_ref[...] * x) * (z * jax.nn.sigmoid(z))
    for g in range(SSD_GROUPS):
        gs = slice(g * SSD_GROUP_WIDTH, (g + 1) * SSD_GROUP_WIDTH)
        yg = y[:, gs]
        yg = yg * lax.rsqrt(jnp.mean(yg * yg, axis=-1, keepdims=True) + NORM_EPS)
        y_ref[:, gs] = (yg * nw_ref[:, gs]).astype(y_ref.dtype)

    @pl.when(c == pl.num_programs(1) - 1)
    def _():
        hout_ref[0] = state_ref[...]


def _ssd_core(xbc, z, dt_raw, h0, tail, conv_w, conv_b, dt_bias, a_log, d_skip, norm_w,
              n_seq, n_chunks, layer, valid_len):
    q = SSD_CHUNK
    t = xbc.shape[0]
    pad_h = LANES - SSD_HEADS
    dtb = jnp.pad(dt_bias, (0, pad_h)).reshape(1, LANES)
    alog = jnp.pad(a_log, (0, pad_h)).reshape(1, LANES)
    dsk = jnp.repeat(d_skip, SSD_HEAD_DIM).reshape(1, SSD_D_INNER)
    tri = jnp.tril(jnp.ones((q, q), F32)).astype(BF16)
    head_of_channel = jnp.arange(SSD_D_INNER) // SSD_HEAD_DIM
    ex = (jnp.arange(LANES)[:, None] == head_of_channel[None, :]).astype(BF16)
    row = lambda s, c: (s * n_chunks + c, 0)
    fixed = lambda s, c: (0, 0)
    seq3 = lambda s, c: (layer * n_seq + s, 0, 0)
    y, hout = pl.pallas_call(
        functools.partial(_ssd_kernel, valid_len=valid_len),
        grid=(n_seq, n_chunks),
        in_specs=[pl.BlockSpec((q, SSD_CONV_DIM), row), pl.BlockSpec((q, SSD_D_INNER), row),
                  pl.BlockSpec((q, LANES), row),
                  pl.BlockSpec((1, SSD_D_INNER, SSD_STATE), seq3),
                  pl.BlockSpec((1, SUBLANES, SSD_CONV_DIM), seq3),
                  pl.BlockSpec((SSD_CONV, SSD_CONV_DIM), fixed), pl.BlockSpec((1, SSD_CONV_DIM), fixed),
                  pl.BlockSpec((1, LANES), fixed), pl.BlockSpec((1, LANES), fixed),
                  pl.BlockSpec((1, SSD_D_INNER), fixed), pl.BlockSpec((1, SSD_D_INNER), fixed),
                  pl.BlockSpec((q, q), fixed), pl.BlockSpec((LANES, SSD_D_INNER), fixed),
                  pl.BlockSpec((SSD_D_INNER, LANES), fixed)],
        out_specs=[pl.BlockSpec((q, SSD_D_INNER), row),
                   pl.BlockSpec((1, SSD_D_INNER, SSD_STATE), lambda s, c: (s, 0, 0))],
        out_shape=[jax.ShapeDtypeStruct((t, SSD_D_INNER), BF16),
                   jax.ShapeDtypeStruct((n_seq, SSD_D_INNER, SSD_STATE), F32)],
        scratch_shapes=[pltpu.VMEM((SSD_D_INNER, SSD_STATE), F32),
                        pltpu.VMEM((q + SUBLANES, SSD_CONV_DIM), F32),
                        pltpu.VMEM((q, SSD_D_INNER), F32)],
        compiler_params=_params("parallel", "arbitrary"),
        name="ssd_core",
    )(xbc, z, dt_raw, h0, tail, conv_w, conv_b.reshape(1, SSD_CONV_DIM), dtb, alog, dsk,
      norm_w.reshape(1, SSD_D_INNER), tri, ex, ex.T)
    return y, hout


def _rope_tables(pos):
    inv = ROPE_THETA ** (-jnp.arange(HALF_HEAD, dtype=F32) / HALF_HEAD)
    ang = pos.astype(F32)[:, None] * inv[None, :]
    cos, sin = jnp.cos(ang), jnp.sin(ang)
    return jnp.concatenate([cos, cos], axis=1), jnp.concatenate([-sin, sin], axis=1)


def _attn_project(u, w_in, rope, tm):
    nq = N_GROUPS * ATTN_WIDTH
    nk = N_GROUPS * KV_WIDTH
    wq = w_in[:, :nq].astype(BF16)
    wk = w_in[:, nq:nq + nk].astype(BF16)
    wv = w_in[:, nq + nk:nq + 2 * nk].astype(BF16)
    wg = w_in[:, nq + 2 * nk:].astype(BF16)
    q = _proj(u, wq, BF16, tm, 512, rope)
    k = _proj(u, wk, F32, tm, 512, rope)
    v = _proj(u, wv, F32, tm, 512)
    gate = _proj(u, wg, F32, tm, 512)
    return q, k, v, gate


def _kv_buffers(k, v, n_seq, seq_len):
    bufs = []
    k4 = k.reshape(n_seq, seq_len, N_GROUPS, KV_HEADS, HEAD_DIM)
    v4 = v.reshape(n_seq, seq_len, N_GROUPS, KV_HEADS, HEAD_DIM)
    for g, (win, _) in enumerate(DILATED_GROUPS):
        keep = min(win, seq_len)
        bufs.append(jnp.stack([k4[:, seq_len - keep:, g], v4[:, seq_len - keep:, g]], axis=2))
    return bufs


def _ssd_project(u, w_in, tm):
    wz = w_in[:, :SSD_D_INNER].astype(BF16)
    wx = w_in[:, SSD_D_INNER:SSD_D_INNER + SSD_CONV_DIM].astype(BF16)
    wdt = jnp.pad(w_in[:, SSD_D_INNER + SSD_CONV_DIM:], ((0, 0), (0, LANES - SSD_HEADS))).astype(BF16)
    z = _proj(u, wz, F32, tm, 512)
    xbc = _proj(u, wx, F32, tm, 512)
    dt_raw = _proj(u, wdt, F32, tm, LANES)
    return z, xbc, dt_raw


def kernel(x_prompt, x_sample, cache_kv_w128, cache_kv_w512, cache_kv_w2048, state_ssm, state_conv,
           norm_pre, norm_post, attn_w_in, attn_w_out, ssd_w_in, ssd_conv_w, ssd_conv_b,
           ssd_dt_bias, ssd_a_log, ssd_d, ssd_norm_w, ssd_w_out):
    n_p, len_p, d = x_prompt.shape
    n_s, len_s, _ = x_sample.shape
    depth = norm_pre.shape[0]
    t_p, t_s = n_p * len_p, n_s * len_s
    hp = x_prompt.reshape(t_p, d)
    hs = x_sample.reshape(t_s, d)
    tm_p, tm_s = 512, 256
    up = _rmsnorm(hp, norm_pre[0], tm_p)
    us = _rmsnorm(hs, norm_pre[0], tm_s)
    rope_p = _rope_tables(jnp.arange(len_p))
    rope_s = _rope_tables(PAST_LEN + jnp.arange(t_s) % len_s)
    caches = (cache_kv_w128, cache_kv_w512, cache_kv_w2048)

    chunk = SSD_CHUNK
    n_ssd = state_ssm.shape[0]
    h0_p = jnp.zeros((n_p, SSD_D_INNER, SSD_STATE), F32)
    tail_p = jnp.zeros((n_p, SUBLANES, SSD_CONV_DIM), F32)
    h0_s = state_ssm.reshape(n_ssd * n_s, SSD_D_INNER, SSD_STATE)
    tail_s = jnp.pad(state_conv, ((0, 0), (0, 0), (SUBLANES - (SSD_CONV - 1), 0), (0, 0)))
    tail_s = tail_s.reshape(n_ssd * n_s, SUBLANES, SSD_CONV_DIM)

    p_kv, s_kv = [[], [], []], [[], [], []]
    p_ssm, p_conv, s_ssm, s_conv = [], [], [], []
    for i in range(depth):
        j = i // 2
        next_w = norm_pre[i + 1] if i + 1 < depth else None
        if i % 2 == 0:
            w_out = attn_w_out[j].astype(BF16)
            q, k, v, gate = _attn_project(up, attn_w_in[j], rope_p, tm_p)
            os_, lses = [], []
            for g, (_, dil) in enumerate(DILATED_GROUPS):
                o, lse = _band_attn(q, k, v, g, dil, n_p, len_p)
                os_.append(o)
                lses.append(lse)
            y = _merge_gate(os_, lses, gate, 256)
            for g, buf in enumerate(_kv_buffers(k, v, n_p, len_p)):
                p_kv[g].append(buf)
            hp, up = _out_proj(y, w_out, hp, norm_post[i], next_w, 256)
            q, k, v, gate = _attn_project(us, attn_w_in[j], rope_s, tm_s)
            os_, lses = _decode_attn(q, k, v, caches, j, n_s, len_s)
            y = _merge_gate(os_, lses, gate, 256)
            new_kv = _kv_buffers(k, v, n_s, len_s)
            for g in range(N_GROUPS):
                s_kv[g].append(jnp.concatenate([caches[g][j][:, len_s:], new_kv[g]], axis=1))
            hs, us = _out_proj(y, w_out, hs, norm_post[i], next_w, 256)
        else:
            w_out = ssd_w_out[j].astype(BF16)
            ssd_args = (ssd_conv_w[j], ssd_conv_b[j], ssd_dt_bias[j], ssd_a_log[j], ssd_d[j], ssd_norm_w[j])
            z, xbc, dt_raw = _ssd_project(up, ssd_w_in[j], tm_p)
            y, h_new = _ssd_core(xbc, z, dt_raw, h0_p, tail_p, *ssd_args,
                                 n_seq=n_p, n_chunks=len_p // chunk, layer=0, valid_len=chunk)
            p_ssm.append(h_new.reshape(n_p, SSD_HEADS, SSD_HEAD_DIM, SSD_STATE))
            p_conv.append(xbc.reshape(n_p, len_p, SSD_CONV_DIM)[:, len_p - (SSD_CONV - 1):])
            hp, up = _out_proj(y, w_out, hp, norm_post[i], next_w, 256)
            z, xbc, dt_raw = _ssd_project(us, ssd_w_in[j], tm_s)
            padc = lambda a_: jnp.pad(a_.reshape(n_s, len_s, -1), ((0, 0), (0, chunk - len_s), (0, 0))
                                      ).reshape(n_s * chunk, -1)
            y, h_new = _ssd_core(padc(xbc), padc(z), padc(dt_raw), h0_s, tail_s, *ssd_args,
                                 n_seq=n_s, n_chunks=1, layer=j, valid_len=len_s)
            y = y.reshape(n_s, chunk, SSD_D_INNER)[:, :len_s].reshape(t_s, SSD_D_INNER)
            s_ssm.append(h_new.reshape(n_s, SSD_HEADS, SSD_HEAD_DIM, SSD_STATE))
            xp = jnp.concatenate([state_conv[j], xbc.reshape(n_s, len_s, SSD_CONV_DIM)], axis=1)
            s_conv.append(xp[:, -(SSD_CONV - 1):])
            hs, us = _out_proj(y, w_out, hs, norm_post[i], next_w, 256)

    return (hp.reshape(n_p, len_p, d), hs.reshape(n_s, len_s, d),
            jnp.stack(p_kv[0]), jnp.stack(p_kv[1]), jnp.stack(p_kv[2]),
            jnp.stack(p_ssm), jnp.stack(p_conv),
            jnp.stack(s_kv[0]), jnp.stack(s_kv[1]), jnp.stack(s_kv[2]),
            jnp.stack(s_ssm), jnp.stack(s_conv))
```

```python
import functools

import jax
import jax.numpy as jnp
from jax import lax
from jax.experimental import pallas as pl
from jax.experimental.pallas import tpu as pltpu

F32 = jnp.float32
BF16 = jnp.bfloat16

D_MODEL = 1024
HEAD_DIM = 128
HALF_HEAD = HEAD_DIM // 2
Q_HEADS = 16
KV_HEADS = 4
REP = Q_HEADS // KV_HEADS
ATTN_WIDTH = Q_HEADS * HEAD_DIM
KV_WIDTH = KV_HEADS * HEAD_DIM
QKV_WIDTH = ATTN_WIDTH + 2 * KV_WIDTH
KV_ROWS = 2 * KV_HEADS
DILATED_GROUPS = ((128, 1), (512, 4), (2048, 16))
N_GROUPS = len(DILATED_GROUPS)
BAND = 128
ROPE_THETA = 10000.0
PAST_LEN = 2048
SSD_D_INNER = 2048
SSD_HEAD_DIM = 64
SSD_HEADS = 32
SSD_GROUPS = 4
SSD_GROUP_WIDTH = SSD_D_INNER // SSD_GROUPS
SSD_STATE = 128
SSD_CONV = 4
SSD_CHUNK = 128
SSD_BC_WIDTH = SSD_GROUPS * SSD_STATE
SSD_CONV_DIM = SSD_D_INNER + 2 * SSD_BC_WIDTH
NORM_EPS = 1e-6
LANES = 128
SUBLANES = 8
VMEM_LIMIT_BYTES = 48 * 1024 * 1024
CACHE_COPIES_IN_FLIGHT = 8
NEG_INF = float("-inf")


def _params(*semantics):
    return pltpu.CompilerParams(dimension_semantics=semantics, vmem_limit_bytes=VMEM_LIMIT_BYTES)


def _nt_dot(a, b):
    return lax.dot_general(a, b, (((1,), (1,)), ((), ())), preferred_element_type=F32)


def _split3(a):
    hi = a.astype(BF16)
    r1 = a - hi.astype(F32)
    mid = r1.astype(BF16)
    lo = (r1 - mid.astype(F32)).astype(BF16)
    return hi, mid, lo


def _sel_dot_right(a, sel3):
    return jnp.dot(jnp.concatenate(_split3(a), axis=1), sel3, preferred_element_type=F32)


def _sel_dot_left(sel3, a):
    return jnp.dot(sel3, jnp.concatenate(_split3(a), axis=0), preferred_element_type=F32)


def _rms(x, w):
    return x * lax.rsqrt(jnp.mean(x * x, axis=-1, keepdims=True) + NORM_EPS) * w


def _silu(x):
    return x * jax.nn.sigmoid(x)


def _rmsnorm_kernel(x_ref, w_ref, o_ref):
    o_ref[...] = _rms(x_ref[...], w_ref[...]).astype(o_ref.dtype)


def _rmsnorm(x, w, tm):
    t, d = x.shape
    return pl.pallas_call(
        _rmsnorm_kernel,
        grid=(t // tm,),
        in_specs=[pl.BlockSpec((tm, d), lambda i: (i, 0)), pl.BlockSpec((1, d), lambda i: (0, 0))],
        out_specs=pl.BlockSpec((tm, d), lambda i: (i, 0)),
        out_shape=jax.ShapeDtypeStruct((t, d), BF16),
        compiler_params=_params("parallel"),
        name="rmsnorm",
    )(x, w.reshape(1, d))


def _proj_kernel(u_ref, w_ref, o_ref):
    o_ref[...] = jnp.dot(u_ref[...], w_ref[...], preferred_element_type=F32).astype(o_ref.dtype)


def _proj(u, w, out_dtype, tm):
    t, k = u.shape
    n = w.shape[1]
    return pl.pallas_call(
        _proj_kernel,
        grid=(t // tm,),
        in_specs=[pl.BlockSpec((tm, k), lambda i: (i, 0)), pl.BlockSpec((k, n), lambda i: (0, 0))],
        out_specs=pl.BlockSpec((tm, n), lambda i: (i, 0)),
        out_shape=jax.ShapeDtypeStruct((t, n), out_dtype),
        compiler_params=_params("parallel"),
        name="proj",
    )(u, w)


def _qkv_kernel(u_ref, w_ref, cos_ref, sin_ref, q_ref, k_ref, v_ref, sc_ref, *, dil):
    acc = jnp.dot(u_ref[...], w_ref[...], preferred_element_type=F32)
    cos = cos_ref[...]
    sin = sin_ref[...]
    n_rot = (ATTN_WIDTH + KV_WIDTH) // HEAD_DIM
    n_heads = QKV_WIDTH // HEAD_DIM
    rows = acc.shape[0] // dil

    def emit(c, r, val):
        if c < Q_HEADS:
            q_ref[:, r * ATTN_WIDTH + c * HEAD_DIM:r * ATTN_WIDTH + (c + 1) * HEAD_DIM] = val.astype(q_ref.dtype)
        elif c < n_rot:
            c0 = r * KV_WIDTH + (c - Q_HEADS) * HEAD_DIM
            k_ref[:, c0:c0 + HEAD_DIM] = val
        else:
            c0 = r * KV_WIDTH + (c - n_rot) * HEAD_DIM
            v_ref[:, c0:c0 + HEAD_DIM] = val

    for c in range(n_heads):
        x = acc[:, c * HEAD_DIM:(c + 1) * HEAD_DIM]
        if c < n_rot:
            x = x * cos + pltpu.roll(x, HALF_HEAD, 1) * sin
        if dil == 1:
            emit(c, 0, x)
        else:
            sc_ref[c] = x
    if dil > 1:
        for r in range(dil):
            for c in range(n_heads):
                emit(c, r, sc_ref[c, pl.ds(r, rows, stride=dil), :])


def _qkv_proj(u, w, rope, dil, tm):
    t, d = u.shape
    cos, sin = rope
    nblk = cos.shape[0] // tm
    rows = tm // dil
    row = lambda i: (i, 0)
    fixed = lambda i: (0, 0)
    return pl.pallas_call(
        functools.partial(_qkv_kernel, dil=dil),
        grid=(t // tm,),
        in_specs=[pl.BlockSpec((tm, d), row), pl.BlockSpec((d, QKV_WIDTH), fixed),
                  pl.BlockSpec((tm, HEAD_DIM), lambda i: (i % nblk, 0)),
                  pl.BlockSpec((tm, HEAD_DIM), lambda i: (i % nblk, 0))],
        out_specs=[pl.BlockSpec((rows, dil * ATTN_WIDTH), row), pl.BlockSpec((rows, dil * KV_WIDTH), row),
                   pl.BlockSpec((rows, dil * KV_WIDTH), row)],
        out_shape=[jax.ShapeDtypeStruct((t // dil, dil * ATTN_WIDTH), BF16),
                   jax.ShapeDtypeStruct((t // dil, dil * KV_WIDTH), F32),
                   jax.ShapeDtypeStruct((t // dil, dil * KV_WIDTH), F32)],
        scratch_shapes=[pltpu.VMEM((QKV_WIDTH // HEAD_DIM, tm, HEAD_DIM), F32)],
        compiler_params=_params("parallel"),
        name="qkv_proj_d%d" % dil,
    )(u, w, cos, sin)


def _band_attn_kernel(q_ref, kp_ref, kc_ref, vp_ref, vc_ref, o_ref, lse_ref):
    mb = pl.program_id(1)
    rows = REP * BAND
    qi = lax.broadcasted_iota(jnp.int32, (rows, 2 * BAND), 0) % BAND
    kj = lax.broadcasted_iota(jnp.int32, (rows, 2 * BAND), 1)
    dist = BAND + qi - kj
    has_prev = mb > 0
    valid = (dist >= 0) & (dist <= BAND) & ((kj >= BAND) | has_prev)
    lane = lax.broadcasted_iota(jnp.int32, (BAND, LANES), 1)
    lse_tile = jnp.zeros((BAND, LANES), F32)
    scale = HEAD_DIM ** -0.5
    for g in range(KV_HEADS):
        cs = slice(g * HEAD_DIM, (g + 1) * HEAD_DIM)
        k2 = jnp.concatenate([kp_ref[:, cs], kc_ref[:, cs]], axis=0).astype(BF16)
        v2 = jnp.concatenate([vp_ref[:, cs], vc_ref[:, cs]], axis=0).astype(BF16)
        q4 = jnp.concatenate(
            [q_ref[:, (g * REP + r) * HEAD_DIM:(g * REP + r + 1) * HEAD_DIM] for r in range(REP)], axis=0)
        s = _nt_dot(q4, k2) * scale
        s = jnp.where(valid, s, NEG_INF)
        m = jnp.max(s, axis=-1, keepdims=True)
        p = jnp.exp(s - m)
        den = jnp.sum(p, axis=-1, keepdims=True)
        o = jnp.dot(p.astype(BF16), v2, preferred_element_type=F32) / den
        lse = m + jnp.log(den)
        for r in range(REP):
            h = g * REP + r
            o_ref[:, h * HEAD_DIM:(h + 1) * HEAD_DIM] = o[r * BAND:(r + 1) * BAND].astype(o_ref.dtype)
            lse_tile = jnp.where(lane == h, lse[r * BAND:(r + 1) * BAND], lse_tile)
    lse_ref[...] = lse_tile


def _band_attn(q, k, v, dil, n_batch, seq_len):
    rows = q.shape[0]
    blocks_per_seq = seq_len // dil // BAND

    def cur_map(s, mb):
        return ((s // dil) * blocks_per_seq + mb, s % dil)

    def prev_map(s, mb):
        return ((s // dil) * blocks_per_seq + jnp.maximum(mb - 1, 0), s % dil)

    return pl.pallas_call(
        _band_attn_kernel,
        grid=(n_batch * dil, blocks_per_seq),
        in_specs=[pl.BlockSpec((BAND, ATTN_WIDTH), cur_map),
                  pl.BlockSpec((BAND, KV_WIDTH), prev_map), pl.BlockSpec((BAND, KV_WIDTH), cur_map),
                  pl.BlockSpec((BAND, KV_WIDTH), prev_map), pl.BlockSpec((BAND, KV_WIDTH), cur_map)],
        out_specs=[pl.BlockSpec((BAND, ATTN_WIDTH), cur_map), pl.BlockSpec((BAND, LANES), cur_map)],
        out_shape=[jax.ShapeDtypeStruct((rows, dil * ATTN_WIDTH), BF16),
                   jax.ShapeDtypeStruct((rows, dil * LANES), F32)],
        compiler_params=_params("parallel", "arbitrary"),
        name="band_attn_d%d" % dil,
    )(q, k, k, v, v)


def _post_mix(y2, h_ref, pw_ref, nw_ref, hn_ref, un_ref):
    hn = h_ref[...] + _rms(y2, pw_ref[...])
    hn_ref[...] = hn
    if un_ref is not None:
        un_ref[...] = _rms(hn, nw_ref[...]).astype(un_ref.dtype)


def _attn_out_kernel(*refs, dils, has_next):
    o_refs, l_refs = refs[0:N_GROUPS], refs[N_GROUPS:2 * N_GROUPS]
    gate_ref, w_ref, h_ref, pw_ref = refs[2 * N_GROUPS:2 * N_GROUPS + 4]
    pos = 2 * N_GROUPS + 4
    nw_ref = refs[pos] if has_next else None
    pos += int(has_next)
    hn_ref = refs[pos]
    un_ref = refs[pos + 1] if has_next else None
    pos += 1 + int(has_next)
    y_ref = refs[pos]
    scratch = list(refs[pos + 1:])
    tm = gate_ref.shape[0]
    o_tok, l_tok = [], []
    for g, dil in enumerate(dils):
        if dil == 1:
            o_tok.append(lambda h, ref=o_refs[g]: ref[:, h * HEAD_DIM:(h + 1) * HEAD_DIM].astype(F32))
            l_tok.append(l_refs[g][...])
            continue
        osc, lsc = scratch.pop(0), scratch.pop(0)
        rows = tm // dil
        for r in range(dil):
            for h in range(Q_HEADS):
                c0 = r * ATTN_WIDTH + h * HEAD_DIM
                osc[h, pl.ds(r, rows, stride=dil), :] = o_refs[g][:, c0:c0 + HEAD_DIM].astype(F32)
            lsc[pl.ds(r, rows, stride=dil), :] = l_refs[g][:, r * LANES:(r + 1) * LANES]
        o_tok.append(lambda h, ref=osc: ref[h])
        l_tok.append(lsc[...])
    m = functools.reduce(jnp.maximum, l_tok)
    es = [jnp.exp(l - m) for l in l_tok]
    den = functools.reduce(lambda a, b: a + b, es)
    ws = [e / den for e in es]
    for h in range(Q_HEADS):
        cs = slice(h * HEAD_DIM, (h + 1) * HEAD_DIM)
        y = functools.reduce(lambda a, b: a + b,
                             [ws[g][:, h:h + 1] * o_tok[g](h) for g in range(N_GROUPS)])
        y_ref[:, cs] = (y * _silu(gate_ref[:, cs])).astype(y_ref.dtype)
    y2 = jnp.dot(y_ref[...], w_ref[...], preferred_element_type=F32)
    _post_mix(y2, h_ref, pw_ref, nw_ref, hn_ref, un_ref)


def _attn_out(os_, lses, dils, gate, w, h, post_w, next_w, tm):
    t, d = h.shape
    row = lambda i: (i, 0)
    fixed = lambda i: (0, 0)
    in_specs = [pl.BlockSpec((tm // dl, dl * ATTN_WIDTH), row) for dl in dils]
    in_specs += [pl.BlockSpec((tm // dl, dl * LANES), row) for dl in dils]
    in_specs += [pl.BlockSpec((tm, ATTN_WIDTH), row), pl.BlockSpec((ATTN_WIDTH, d), fixed),
                 pl.BlockSpec((tm, d), row), pl.BlockSpec((1, d), fixed)]
    args = list(os_) + list(lses) + [gate, w, h, post_w.reshape(1, d)]
    out_specs = [pl.BlockSpec((tm, d), row)]
    out_shape = [jax.ShapeDtypeStruct((t, d), F32)]
    if next_w is not None:
        in_specs.append(pl.BlockSpec((1, d), fixed))
        args.append(next_w.reshape(1, d))
        out_specs.append(pl.BlockSpec((tm, d), row))
        out_shape.append(jax.ShapeDtypeStruct((t, d), BF16))
    scratch = [pltpu.VMEM((tm, ATTN_WIDTH), BF16)]
    for dl in dils:
        if dl > 1:
            scratch += [pltpu.VMEM((Q_HEADS, tm, HEAD_DIM), F32), pltpu.VMEM((tm, LANES), F32)]
    res = pl.pallas_call(
        functools.partial(_attn_out_kernel, dils=tuple(dils), has_next=next_w is not None),
        grid=(t // tm,), in_specs=in_specs, out_specs=out_specs, out_shape=out_shape,
        scratch_shapes=scratch, compiler_params=_params("parallel"), name="attn_out")(*args)
    return (res[0], res[1]) if next_w is not None else (res[0], None)


def _out_kernel(*refs, has_next):
    y_ref, w_ref, h_ref, pw_ref = refs[0:4]
    nw_ref = refs[4] if has_next else None
    hn_ref = refs[4 + int(has_next)]
    un_ref = refs[5 + int(has_next)] if has_next else None
    y2 = jnp.dot(y_ref[...], w_ref[...], preferred_element_type=F32)
    _post_mix(y2, h_ref, pw_ref, nw_ref, hn_ref, un_ref)


def _out_proj(y, w, h, post_w, next_w, tm):
    t, k = y.shape
    d = w.shape[1]
    row = lambda i: (i, 0)
    fixed = lambda i: (0, 0)
    in_specs = [pl.BlockSpec((tm, k), row), pl.BlockSpec((k, d), fixed), pl.BlockSpec((tm, d), row),
                pl.BlockSpec((1, d), fixed)]
    args = [y, w, h, post_w.reshape(1, d)]
    out_specs = [pl.BlockSpec((tm, d), row)]
    out_shape = [jax.ShapeDtypeStruct((t, d), F32)]
    if next_w is not None:
        in_specs.append(pl.BlockSpec((1, d), fixed))
        args.append(next_w.reshape(1, d))
        out_specs.append(pl.BlockSpec((tm, d), row))
        out_shape.append(jax.ShapeDtypeStruct((t, d), BF16))
    res = pl.pallas_call(
        functools.partial(_out_kernel, has_next=next_w is not None),
        grid=(t // tm,), in_specs=in_specs, out_specs=out_specs, out_shape=out_shape,
        compiler_params=_params("parallel"), name="out_proj")(*args)
    return (res[0], res[1]) if next_w is not None else (res[0], None)


def _decode_attn_kernel(q_ref, k_ref, v_ref, c0_ref, c1_ref, c2_ref, o_ref, lse_ref, *, n_tok):
    rows = n_tok * REP
    scale = HEAD_DIM ** -0.5
    row_t = lax.broadcasted_iota(jnp.int32, (rows, BAND), 0) // REP
    col = lax.broadcasted_iota(jnp.int32, (rows, BAND), 1)
    row_t1 = row_t[:, :1]
    lane = lax.broadcasted_iota(jnp.int32, (rows, LANES), 1)
    lse_tile = jnp.zeros((rows, LANES), F32)
    crefs = (c0_ref, c1_ref, c2_ref)
    for g in range(N_GROUPS):
        cref = crefs[g]
        for kh in range(KV_HEADS):
            q16 = q_ref[0, g * KV_HEADS + kh]
            knew = k_ref[0, g][:, kh * HEAD_DIM:(kh + 1) * HEAD_DIM]
            vnew = v_ref[0, g][:, kh * HEAD_DIM:(kh + 1) * HEAD_DIM]
            s_tiles, v_tiles = [], []
            if g == 0:
                kc = cref[:, kh, :].astype(BF16)
                v_tiles.append(cref[:, KV_HEADS + kh, :].astype(BF16))
                s_tiles.append(jnp.where(col >= row_t, _nt_dot(q16, kc) * scale, NEG_INF))
            else:
                for t in range(n_tok):
                    kc = cref[:, t * KV_ROWS + kh, :].astype(BF16)
                    v_tiles.append(cref[:, t * KV_ROWS + KV_HEADS + kh, :].astype(BF16))
                    s_tiles.append(jnp.where(row_t == t, _nt_dot(q16, kc) * scale, NEG_INF))
            qf = q16.astype(F32)
            s_new = []
            for t in range(n_tok):
                sn = jnp.sum(qf * knew[t:t + 1, :], axis=-1, keepdims=True) * scale
                ok = (row_t1 >= t) if g == 0 else (row_t1 == t)
                s_new.append(jnp.where(ok, sn, NEG_INF))
            m = s_new[0]
            for sn in s_new[1:]:
                m = jnp.maximum(m, sn)
            for st in s_tiles:
                m = jnp.maximum(m, jnp.max(st, axis=-1, keepdims=True))
            den = jnp.zeros((rows, 1), F32)
            acc = jnp.zeros((rows, HEAD_DIM), F32)
            for st, vt in zip(s_tiles, v_tiles):
                p = jnp.exp(st - m)
                den = den + jnp.sum(p, axis=-1, keepdims=True)
                acc = acc + jnp.dot(p.astype(BF16), vt, preferred_element_type=F32)
            for t in range(n_tok):
                pn = jnp.exp(s_new[t] - m)
                den = den + pn
                acc = acc + pn * vnew[t:t + 1, :]
            o_ref[0, g * KV_HEADS + kh] = (acc / den).astype(o_ref.dtype)
            lse_tile = jnp.where(lane == g * KV_HEADS + kh, m + jnp.log(den), lse_tile)
    lse_ref[0] = lse_tile


def _decode_attn(qs, ks, vs, cache_views, layer, n_seq, n_tok):
    rows = n_tok * REP
    qh = jnp.stack(qs, axis=1).reshape(n_seq, n_tok, N_GROUPS, KV_HEADS, REP, HEAD_DIM)
    qh = jnp.transpose(qh, (0, 2, 3, 1, 4, 5)).reshape(n_seq, N_GROUPS * KV_HEADS, rows, HEAD_DIM)
    k4 = jnp.stack(ks, axis=1).reshape(n_seq, n_tok, N_GROUPS, KV_WIDTH).transpose(0, 2, 1, 3)
    v4 = jnp.stack(vs, axis=1).reshape(n_seq, n_tok, N_GROUPS, KV_WIDTH).transpose(0, 2, 1, 3)
    in_specs = [pl.BlockSpec((1, N_GROUPS * KV_HEADS, rows, HEAD_DIM), lambda b: (b, 0, 0, 0)),
                pl.BlockSpec((1, N_GROUPS, n_tok, KV_WIDTH), lambda b: (b, 0, 0, 0)),
                pl.BlockSpec((1, N_GROUPS, n_tok, KV_WIDTH), lambda b: (b, 0, 0, 0))]
    cargs = []
    for g, (win, dil) in enumerate(DILATED_GROUPS):
        c = cache_views[g]
        n_layers = c.shape[0]
        assert c.shape[2] == win * KV_ROWS and win // dil == BAND and (dil == 1 or dil >= n_tok)
        cargs.append(c.reshape(n_layers, n_seq, BAND, dil * KV_ROWS, HEAD_DIM))
        in_specs.append(pl.BlockSpec((None, None, BAND, min(dil, n_tok) * KV_ROWS, HEAD_DIM),
                                     lambda b: (layer, b, 0, 0, 0)))
    o, lse = pl.pallas_call(
        functools.partial(_decode_attn_kernel, n_tok=n_tok),
        grid=(n_seq,),
        in_specs=in_specs,
        out_specs=[pl.BlockSpec((1, N_GROUPS * KV_HEADS, rows, HEAD_DIM), lambda b: (b, 0, 0, 0)),
                   pl.BlockSpec((1, rows, LANES), lambda b: (b, 0, 0))],
        out_shape=[jax.ShapeDtypeStruct((n_seq, N_GROUPS * KV_HEADS, rows, HEAD_DIM), BF16),
                   jax.ShapeDtypeStruct((n_seq, rows, LANES), F32)],
        compiler_params=_params("parallel"),
        name="decode_attn",
    )(qh, k4, v4, *cargs)
    o = o.reshape(n_seq, N_GROUPS, KV_HEADS, n_tok, REP, HEAD_DIM)
    o = jnp.transpose(o, (1, 0, 3, 2, 4, 5)).reshape(N_GROUPS, n_seq * n_tok, ATTN_WIDTH)
    lse = lse[:, :, :N_GROUPS * KV_HEADS].reshape(n_seq, n_tok, REP, N_GROUPS, KV_HEADS)
    lse = jnp.transpose(lse, (3, 0, 1, 4, 2)).reshape(N_GROUPS, n_seq * n_tok, Q_HEADS)
    lse = jnp.pad(lse, ((0, 0), (0, 0), (0, LANES - Q_HEADS)))
    return [o[g] for g in range(N_GROUPS)], [lse[g] for g in range(N_GROUPS)]


def _cache_shift_kernel(*refs):
    news, caches = refs[0:N_GROUPS], refs[N_GROUPS:2 * N_GROUPS]
    outs, sem = refs[2 * N_GROUPS:3 * N_GROUPS], refs[3 * N_GROUPS]
    n_blocks = caches[0].shape[0]

    def copies(i):
        cps = []
        for g in range(N_GROUPS):
            rows = caches[g].shape[1]
            new_rows = news[g].shape[1]
            cps.append(pltpu.make_async_copy(caches[g].at[i, pl.ds(new_rows, rows - new_rows)],
                                             outs[g].at[i, pl.ds(0, rows - new_rows)], sem.at[g]))
            cps.append(pltpu.make_async_copy(news[g].at[i],
                                             outs[g].at[i, pl.ds(rows - new_rows, new_rows)], sem.at[g]))
        return cps

    def wait(i, carry):
        for cp in copies(i):
            cp.wait()
        return carry

    def step(i, carry):
        @pl.when(i >= CACHE_COPIES_IN_FLIGHT)
        def _():
            wait(i - CACHE_COPIES_IN_FLIGHT, carry)

        for cp in copies(i):
            cp.start()
        return carry

    lax.fori_loop(0, n_blocks, step, 0)
    lax.fori_loop(max(n_blocks - CACHE_COPIES_IN_FLIGHT, 0), n_blocks, wait, 0)


def _cache_shift(news, cache_views):
    any_spec = pl.BlockSpec(memory_space=pl.ANY)
    flat = lambda a: a.reshape((a.shape[0] * a.shape[1],) + a.shape[2:])
    outs = pl.pallas_call(
        _cache_shift_kernel,
        in_specs=[any_spec] * (2 * N_GROUPS),
        out_specs=[any_spec] * N_GROUPS,
        out_shape=[jax.ShapeDtypeStruct(flat(c).shape, c.dtype) for c in cache_views],
        scratch_shapes=[pltpu.SemaphoreType.DMA((N_GROUPS,))],
        name="cache_shift",
    )(*[flat(a) for a in news], *[flat(c) for c in cache_views])
    return [o.reshape(c.shape) for o, c in zip(outs, cache_views)]


def _ssd_kernel(xbc_ref, z_ref, dt_ref, h0_ref, tail_ref, cw_ref, cb_ref, dtb_ref, alog_ref, dsk_ref,
                nw_ref, tri_ref, ex_ref, ext_ref, y_ref, hout_ref, state_ref, xp_ref, yacc_ref, *, valid_len):
    c = pl.program_id(1)
    q = SSD_CHUNK
    pad = SUBLANES

    @pl.when(c == 0)
    def _():
        state_ref[...] = h0_ref[0]
        xp_ref[0:pad, :] = tail_ref[0]

    xp_ref[pad:pad + q, :] = xbc_ref[...]
    conv = cb_ref[...]
    for k in range(SSD_CONV):
        off = pad - (SSD_CONV - 1) + k
        conv = conv + cw_ref[k:k + 1, :] * xp_ref[off:off + q, :]
    xp_ref[0:pad, :] = xbc_ref[q - pad:q, :]
    xc = _silu(conv)
    x = xc[:, :SSD_D_INNER]
    bm = xc[:, SSD_D_INNER:SSD_D_INNER + SSD_BC_WIDTH].astype(BF16)
    cm = xc[:, SSD_D_INNER + SSD_BC_WIDTH:].astype(BF16)

    dt = jax.nn.softplus(dt_ref[...] + dtb_ref[...])
    if valid_len < q:
        trow = lax.broadcasted_iota(jnp.int32, (q, LANES), 0)
        dt = jnp.where(trow < valid_len, dt, 0.0)
    a = -jnp.exp(alog_ref[...])
    tri = tri_ref[...]
    acs = _sel_dot_left(tri, dt * a)
    acs_t = acs.T
    dt_t = dt.T
    acs_last = acs[q - 1:q, :]
    ex = ex_ref[...]
    e_in = _sel_dot_right(jnp.exp(acs), ex)
    e_end = _sel_dot_right(jnp.exp(acs_last - acs) * dt, ex)
    xs = x * e_end
    cd_col = jnp.exp(acs_t[:, q - 1:q])
    cd = _sel_dot_left(ext_ref[...], jnp.broadcast_to(cd_col, (LANES, SSD_STATE)))
    tmask = lax.broadcasted_iota(jnp.int32, (q, q), 0) >= lax.broadcasted_iota(jnp.int32, (q, q), 1)
    lane = lax.broadcasted_iota(jnp.int32, (q, LANES), 1)
    for g in range(SSD_GROUPS):
        gs = slice(g * SSD_GROUP_WIDTH, (g + 1) * SSD_GROUP_WIDTH)
        bg = bm[:, g * SSD_STATE:(g + 1) * SSD_STATE]
        cg = cm[:, g * SSD_STATE:(g + 1) * SSD_STATE]
        h_g = state_ref[gs, :]
        y_off = _nt_dot(cg, h_g.astype(BF16)) * e_in[:, gs]
        new_states = jnp.dot(xs[:, gs].T.astype(BF16), bg, preferred_element_type=F32)
        state_ref[gs, :] = cd[gs, :] * h_g + new_states
        cb = _nt_dot(cg, bg)
        heads_per_group = SSD_HEADS // SSD_GROUPS
        for pr in range(heads_per_group // 2):
            e0 = g * heads_per_group + 2 * pr
            wts = []
            for e in (e0, e0 + 1):
                seg = acs[:, e:e + 1] - acs_t[e:e + 1, :]
                decay = jnp.exp(jnp.where(tmask, seg, NEG_INF))
                wts.append((cb * decay * dt_t[e:e + 1, :]).astype(BF16))
            x2 = x[:, e0 * SSD_HEAD_DIM:(e0 + 2) * SSD_HEAD_DIM]
            rhs = jnp.concatenate([jnp.where(lane < SSD_HEAD_DIM, x2, 0.0),
                                   jnp.where(lane >= SSD_HEAD_DIM, x2, 0.0)], axis=0).astype(BF16)
            y_diag = jnp.dot(jnp.concatenate(wts, axis=1), rhs, preferred_element_type=F32)
            ls = slice(e0 * SSD_HEAD_DIM, (e0 + 2) * SSD_HEAD_DIM)
            yacc_ref[:, ls] = y_diag + y_off[:, (2 * pr) * SSD_HEAD_DIM:(2 * pr + 2) * SSD_HEAD_DIM]

    y = (yacc_ref[...] + dsk_ref[...] * x) * _silu(z_ref[...])
    for g in range(SSD_GROUPS):
        gs = slice(g * SSD_GROUP_WIDTH, (g + 1) * SSD_GROUP_WIDTH)
        yg = y[:, gs]
        yg = yg * lax.rsqrt(jnp.mean(yg * yg, axis=-1, keepdims=True) + NORM_EPS)
        y_ref[:, gs] = (yg * nw_ref[:, gs]).astype(y_ref.dtype)

    @pl.when(c == pl.num_programs(1) - 1)
    def _():
        hout_ref[0] = state_ref[...]


def _ssd_core(xbc, z, dt_raw, h0, tail, conv_w, conv_b, dt_bias, a_log, d_skip, norm_w,
              n_seq, n_chunks, layer, valid_len):
    q = SSD_CHUNK
    t = xbc.shape[0]
    pad_h = LANES - SSD_HEADS
    dtb = jnp.pad(dt_bias, (0, pad_h)).reshape(1, LANES)
    alog = jnp.pad(a_log, (0, pad_h)).reshape(1, LANES)
    dsk = jnp.repeat(d_skip, SSD_HEAD_DIM).reshape(1, SSD_D_INNER)
    tri = jnp.tril(jnp.ones((q, q), F32)).astype(BF16)
    tri3 = jnp.concatenate([tri] * 3, axis=1)
    head_of_channel = jnp.arange(SSD_D_INNER) // SSD_HEAD_DIM
    ex = (jnp.arange(LANES)[:, None] == head_of_channel[None, :]).astype(BF16)
    ex3 = jnp.concatenate([ex] * 3, axis=0)
    ext3 = jnp.concatenate([ex.T] * 3, axis=1)
    row = lambda s, c: (s * n_chunks + c, 0)
    fixed = lambda s, c: (0, 0)
    seq3 = lambda s, c: (layer * n_seq + s, 0, 0)
    y, hout = pl.pallas_call(
        functools.partial(_ssd_kernel, valid_len=valid_len),
        grid=(n_seq, n_chunks),
        in_specs=[pl.BlockSpec((q, SSD_CONV_DIM), row), pl.BlockSpec((q, SSD_D_INNER), row),
                  pl.BlockSpec((q, LANES), row),
                  pl.BlockSpec((1, SSD_D_INNER, SSD_STATE), seq3),
                  pl.BlockSpec((1, SUBLANES, SSD_CONV_DIM), seq3),
                  pl.BlockSpec((SSD_CONV, SSD_CONV_DIM), fixed), pl.BlockSpec((1, SSD_CONV_DIM), fixed),
                  pl.BlockSpec((1, LANES), fixed), pl.BlockSpec((1, LANES), fixed),
                  pl.BlockSpec((1, SSD_D_INNER), fixed), pl.BlockSpec((1, SSD_D_INNER), fixed),
                  pl.BlockSpec((q, 3 * q), fixed), pl.BlockSpec((3 * LANES, SSD_D_INNER), fixed),
                  pl.BlockSpec((SSD_D_INNER, 3 * LANES), fixed)],
        out_specs=[pl.BlockSpec((q, SSD_D_INNER), row),
                   pl.BlockSpec((1, SSD_D_INNER, SSD_STATE), lambda s, c: (s, 0, 0))],
        out_shape=[jax.ShapeDtypeStruct((t, SSD_D_INNER), BF16),
                   jax.ShapeDtypeStruct((n_seq, SSD_D_INNER, SSD_STATE), F32)],
        scratch_shapes=[pltpu.VMEM((SSD_D_INNER, SSD_STATE), F32),
                        pltpu.VMEM((q + SUBLANES, SSD_CONV_DIM), F32),
                        pltpu.VMEM((q, SSD_D_INNER), F32)],
        compiler_params=_params("parallel", "arbitrary"),
        name="ssd_core",
    )(xbc, z, dt_raw, h0, tail, conv_w, conv_b.reshape(1, SSD_CONV_DIM), dtb, alog, dsk,
      norm_w.reshape(1, SSD_D_INNER), tri3, ex3, ext3)
    return y, hout


def _rope_tables(pos):
    inv = ROPE_THETA ** (-jnp.arange(HALF_HEAD, dtype=F32) / HALF_HEAD)
    ang = pos.astype(F32)[:, None] * inv[None, :]
    cos, sin = jnp.cos(ang), jnp.sin(ang)
    return jnp.concatenate([cos, cos], axis=1), jnp.concatenate([-sin, sin], axis=1)


def _attn_weights(w_in):
    nq = N_GROUPS * ATTN_WIDTH
    nk = N_GROUPS * KV_WIDTH
    w = w_in.astype(BF16)
    groups = []
    for g in range(N_GROUPS):
        groups.append(jnp.concatenate(
            [w[:, g * ATTN_WIDTH:(g + 1) * ATTN_WIDTH],
             w[:, nq + g * KV_WIDTH:nq + (g + 1) * KV_WIDTH],
             w[:, nq + nk + g * KV_WIDTH:nq + nk + (g + 1) * KV_WIDTH]], axis=1))
    return groups, w[:, nq + 2 * nk:]


def _prompt_kv_buffers(ks, vs, n_seq, seq_len):
    bufs = []
    for g, (win, dil) in enumerate(DILATED_GROUPS):
        keep = min(win, seq_len)
        rows_per_seq = seq_len // dil

        def last(a):
            a = a.reshape(n_seq, rows_per_seq, dil * KV_WIDTH)[:, rows_per_seq - keep // dil:]
            return a.reshape(n_seq, keep, KV_HEADS, HEAD_DIM)

        bufs.append(jnp.stack([last(ks[g]), last(vs[g])], axis=2))
    return bufs


def _ssd_project(u, w_in, tm):
    wz = w_in[:, :SSD_D_INNER].astype(BF16)
    wx = w_in[:, SSD_D_INNER:SSD_D_INNER + SSD_CONV_DIM].astype(BF16)
    wdt = jnp.pad(w_in[:, SSD_D_INNER + SSD_CONV_DIM:], ((0, 0), (0, LANES - SSD_HEADS))).astype(BF16)
    z = _proj(u, wz, F32, tm)
    xbc = _proj(u, wx, F32, tm)
    dt_raw = _proj(u, wdt, F32, tm)
    return z, xbc, dt_raw


def kernel(x_prompt, x_sample, cache_kv_w128, cache_kv_w512, cache_kv_w2048, state_ssm, state_conv,
           norm_pre, norm_post, attn_w_in, attn_w_out, ssd_w_in, ssd_conv_w, ssd_conv_b,
           ssd_dt_bias, ssd_a_log, ssd_d, ssd_norm_w, ssd_w_out):
    n_p, len_p, d = x_prompt.shape
    n_s, len_s, _ = x_sample.shape
    depth = norm_pre.shape[0]
    t_p, t_s = n_p * len_p, n_s * len_s
    hp = x_prompt.reshape(t_p, d)
    hs = x_sample.reshape(t_s, d)
    tm = 256
    up = _rmsnorm(hp, norm_pre[0], tm)
    us = _rmsnorm(hs, norm_pre[0], tm)
    rope_p = _rope_tables(jnp.arange(len_p))
    rope_s = _rope_tables(PAST_LEN + jnp.arange(t_s) % len_s)
    caches = (cache_kv_w128, cache_kv_w512, cache_kv_w2048)
    cache_views = [c.reshape(c.shape[0], c.shape[1], c.shape[2] * KV_ROWS, HEAD_DIM) for c in caches]
    dils = [dil for _, dil in DILATED_GROUPS]

    chunk = SSD_CHUNK
    n_ssd = state_ssm.shape[0]
    h0_p = jnp.zeros((n_p, SSD_D_INNER, SSD_STATE), F32)
    tail_p = jnp.zeros((n_p, SUBLANES, SSD_CONV_DIM), F32)
    h0_s = state_ssm.reshape(n_ssd * n_s, SSD_D_INNER, SSD_STATE)
    tail_s = jnp.pad(state_conv, ((0, 0), (0, 0), (SUBLANES - (SSD_CONV - 1), 0), (0, 0)))
    tail_s = tail_s.reshape(n_ssd * n_s, SUBLANES, SSD_CONV_DIM)

    p_kv = [[], [], []]
    new_kv = [[], [], []]
    p_ssm, p_conv, s_ssm, s_conv = [], [], [], []
    for i in range(depth):
        j = i // 2
        next_w = norm_pre[i + 1] if i + 1 < depth else None
        if i % 2 == 0:
            w_out = attn_w_out[j].astype(BF16)
            w_groups, w_gate = _attn_weights(attn_w_in[j])
            os_, lses, ks, vs = [], [], [], []
            for g, dil in enumerate(dils):
                q, k, v = _qkv_proj(up, w_groups[g], rope_p, dil, tm)
                o, lse = _band_attn(q, k, v, dil, n_p, len_p)
                os_.append(o)
                lses.append(lse)
                ks.append(k)
                vs.append(v)
            gate = _proj(up, w_gate, F32, tm)
            for g, buf in enumerate(_prompt_kv_buffers(ks, vs, n_p, len_p)):
                p_kv[g].append(buf)
            hp, up = _attn_out(os_, lses, dils, gate, w_out, hp, norm_post[i], next_w, tm)
            qs, ks, vs = [], [], []
            for g in range(N_GROUPS):
                q, k, v = _qkv_proj(us, w_groups[g], rope_s, 1, tm)
                qs.append(q)
                ks.append(k)
                vs.append(v)
            gate = _proj(us, w_gate, F32, tm)
            os_, lses = _decode_attn(qs, ks, vs, cache_views, j, n_s, len_s)
            for g in range(N_GROUPS):
                new_kv[g].append(jnp.stack([ks[g].reshape(n_s, len_s, KV_HEADS, HEAD_DIM),
                                            vs[g].reshape(n_s, len_s, KV_HEADS, HEAD_DIM)], axis=2
                                           ).reshape(n_s, len_s * KV_ROWS, HEAD_DIM))
            hs, us = _attn_out(os_, lses, [1] * N_GROUPS, gate, w_out, hs, norm_post[i], next_w, tm)
        else:
            w_out = ssd_w_out[j].astype(BF16)
            ssd_args = (ssd_conv_w[j], ssd_conv_b[j], ssd_dt_bias[j], ssd_a_log[j], ssd_d[j], ssd_norm_w[j])
            z, xbc, dt_raw = _ssd_project(up, ssd_w_in[j], tm)
            y, h_new = _ssd_core(xbc, z, dt_raw, h0_p, tail_p, *ssd_args,
                                 n_seq=n_p, n_chunks=len_p // chunk, layer=0, valid_len=chunk)
            p_ssm.append(h_new.reshape(n_p, SSD_HEADS, SSD_HEAD_DIM, SSD_STATE))
            p_conv.append(xbc.reshape(n_p, len_p, SSD_CONV_DIM)[:, len_p - (SSD_CONV - 1):])
            hp, up = _out_proj(y, w_out, hp, norm_post[i], next_w, tm)
            z, xbc, dt_raw = _ssd_project(us, ssd_w_in[j], tm)
            padc = lambda a_: jnp.pad(a_.reshape(n_s, len_s, -1), ((0, 0), (0, chunk - len_s), (0, 0))
                                      ).reshape(n_s * chunk, -1)
            y, h_new = _ssd_core(padc(xbc), padc(z), padc(dt_raw), h0_s, tail_s, *ssd_args,
                                 n_seq=n_s, n_chunks=1, layer=j, valid_len=len_s)
            y = y.reshape(n_s, chunk, SSD_D_INNER)[:, :len_s].reshape(t_s, SSD_D_INNER)
            s_ssm.append(h_new.reshape(n_s, SSD_HEADS, SSD_HEAD_DIM, SSD_STATE))
            xp = jnp.concatenate([state_conv[j], xbc.reshape(n_s, len_s, SSD_CONV_DIM)], axis=1)
            s_conv.append(xp[:, -(SSD_CONV - 1):])
            hs, us = _out_proj(y, w_out, hs, norm_post[i], next_w, tm)

    s_kv = _cache_shift([jnp.stack(n) for n in new_kv], cache_views)
    s_kv = [a.reshape(c.shape) for a, c in zip(s_kv, caches)]
    return (hp.reshape(n_p, len_p, d), hs.reshape(n_s, len_s, d),
            jnp.stack(p_kv[0]), jnp.stack(p_kv[1]), jnp.stack(p_kv[2]),
            jnp.stack(p_ssm), jnp.stack(p_conv),
            s_kv[0], s_kv[1], s_kv[2],
            jnp.stack(s_ssm), jnp.stack(s_conv))
```

```python
import functools

import jax
import jax.numpy as jnp
from jax import lax
from jax.experimental import pallas as pl
from jax.experimental.pallas import tpu as pltpu

F32 = jnp.float32
BF16 = jnp.bfloat16

D_MODEL = 1024
HEAD_DIM = 128
HALF_HEAD = HEAD_DIM // 2
Q_HEADS = 16
KV_HEADS = 4
REP = Q_HEADS // KV_HEADS
ATTN_WIDTH = Q_HEADS * HEAD_DIM
KV_WIDTH = KV_HEADS * HEAD_DIM
QKV_WIDTH = ATTN_WIDTH + 2 * KV_WIDTH
KV_ROWS = 2 * KV_HEADS
DILATED_GROUPS = ((128, 1), (512, 4), (2048, 16))
N_GROUPS = len(DILATED_GROUPS)
BAND = 128
ROPE_THETA = 10000.0
PAST_LEN = 2048
SSD_D_INNER = 2048
SSD_HEAD_DIM = 64
SSD_HEADS = 32
SSD_GROUPS = 4
SSD_GROUP_WIDTH = SSD_D_INNER // SSD_GROUPS
SSD_STATE = 128
SSD_CONV = 4
SSD_CHUNK = 128
SSD_BC_WIDTH = SSD_GROUPS * SSD_STATE
SSD_CONV_DIM = SSD_D_INNER + 2 * SSD_BC_WIDTH
NORM_EPS = 1e-6
LANES = 128
SUBLANES = 8
VMEM_LIMIT_BYTES = 48 * 1024 * 1024
CACHE_SHIFT_SLABS = 128
NEG_INF = float("-inf")


def _params(*semantics):
    return pltpu.CompilerParams(dimension_semantics=semantics, vmem_limit_bytes=VMEM_LIMIT_BYTES)


def _nt_dot(a, b):
    return lax.dot_general(a, b, (((1,), (1,)), ((), ())), preferred_element_type=F32)


def _split3(a):
    hi = a.astype(BF16)
    r1 = a - hi.astype(F32)
    mid = r1.astype(BF16)
    lo = (r1 - mid.astype(F32)).astype(BF16)
    return hi, mid, lo


def _sel_dot_right(a, sel3):
    return jnp.dot(jnp.concatenate(_split3(a), axis=1), sel3, preferred_element_type=F32)


def _sel_dot_left(sel3, a):
    return jnp.dot(sel3, jnp.concatenate(_split3(a), axis=0), preferred_element_type=F32)


def _rms(x, w):
    return x * lax.rsqrt(jnp.mean(x * x, axis=-1, keepdims=True) + NORM_EPS) * w


def _silu(x):
    return x * jax.nn.sigmoid(x)


def _rmsnorm_kernel(x_ref, w_ref, o_ref):
    o_ref[...] = _rms(x_ref[...], w_ref[...]).astype(o_ref.dtype)


def _rmsnorm(x, w, tm):
    t, d = x.shape
    return pl.pallas_call(
        _rmsnorm_kernel,
        grid=(t // tm,),
        in_specs=[pl.BlockSpec((tm, d), lambda i: (i, 0)), pl.BlockSpec((1, d), lambda i: (0, 0))],
        out_specs=pl.BlockSpec((tm, d), lambda i: (i, 0)),
        out_shape=jax.ShapeDtypeStruct((t, d), BF16),
        compiler_params=_params("parallel"),
        name="rmsnorm",
    )(x, w.reshape(1, d))


def _proj_kernel(u_ref, w_ref, o_ref):
    o_ref[...] = jnp.dot(u_ref[...], w_ref[...], preferred_element_type=F32).astype(o_ref.dtype)


def _proj(u, w, out_dtype, tm):
    t, k = u.shape
    n = w.shape[1]
    return pl.pallas_call(
        _proj_kernel,
        grid=(t // tm,),
        in_specs=[pl.BlockSpec((tm, k), lambda i: (i, 0)), pl.BlockSpec((k, n), lambda i: (0, 0))],
        out_specs=pl.BlockSpec((tm, n), lambda i: (i, 0)),
        out_shape=jax.ShapeDtypeStruct((t, n), out_dtype),
        compiler_params=_params("parallel"),
        name="proj",
    )(u, w)


def _qkv_kernel(u_ref, w_ref, cos_ref, sin_ref, q_ref, k_ref, v_ref, sc_ref, *, dil):
    acc = jnp.dot(u_ref[...], w_ref[...], preferred_element_type=F32)
    cos = cos_ref[...]
    sin = sin_ref[...]
    n_rot = (ATTN_WIDTH + KV_WIDTH) // HEAD_DIM
    n_heads = QKV_WIDTH // HEAD_DIM
    rows = acc.shape[0] // dil

    def emit(c, r, val):
        if c < Q_HEADS:
            q_ref[:, r * ATTN_WIDTH + c * HEAD_DIM:r * ATTN_WIDTH + (c + 1) * HEAD_DIM] = val.astype(q_ref.dtype)
        elif c < n_rot:
            c0 = r * KV_WIDTH + (c - Q_HEADS) * HEAD_DIM
            k_ref[:, c0:c0 + HEAD_DIM] = val
        else:
            c0 = r * KV_WIDTH + (c - n_rot) * HEAD_DIM
            v_ref[:, c0:c0 + HEAD_DIM] = val

    for c in range(n_heads):
        x = acc[:, c * HEAD_DIM:(c + 1) * HEAD_DIM]
        if c < n_rot:
            x = x * cos + pltpu.roll(x, HALF_HEAD, 1) * sin
        if dil == 1:
            emit(c, 0, x)
        else:
            sc_ref[c] = x
    if dil > 1:
        for r in range(dil):
            for c in range(n_heads):
                emit(c, r, sc_ref[c, pl.ds(r, rows, stride=dil), :])


def _qkv_proj(u, w, rope, dil, tm):
    t, d = u.shape
    cos, sin = rope
    nblk = cos.shape[0] // tm
    rows = tm // dil
    row = lambda i: (i, 0)
    fixed = lambda i: (0, 0)
    return pl.pallas_call(
        functools.partial(_qkv_kernel, dil=dil),
        grid=(t // tm,),
        in_specs=[pl.BlockSpec((tm, d), row), pl.BlockSpec((d, QKV_WIDTH), fixed),
                  pl.BlockSpec((tm, HEAD_DIM), lambda i: (i % nblk, 0)),
                  pl.BlockSpec((tm, HEAD_DIM), lambda i: (i % nblk, 0))],
        out_specs=[pl.BlockSpec((rows, dil * ATTN_WIDTH), row), pl.BlockSpec((rows, dil * KV_WIDTH), row),
                   pl.BlockSpec((rows, dil * KV_WIDTH), row)],
        out_shape=[jax.ShapeDtypeStruct((t // dil, dil * ATTN_WIDTH), BF16),
                   jax.ShapeDtypeStruct((t // dil, dil * KV_WIDTH), F32),
                   jax.ShapeDtypeStruct((t // dil, dil * KV_WIDTH), F32)],
        scratch_shapes=[pltpu.VMEM((QKV_WIDTH // HEAD_DIM, tm, HEAD_DIM), F32)],
        compiler_params=_params("parallel"),
        name="qkv_proj_d%d" % dil,
    )(u, w, cos, sin)


def _band_attn_kernel(q_ref, kp_ref, kc_ref, vp_ref, vc_ref, o_ref, lse_ref):
    mb = pl.program_id(1)
    rows = REP * BAND
    qi = lax.broadcasted_iota(jnp.int32, (rows, 2 * BAND), 0) % BAND
    kj = lax.broadcasted_iota(jnp.int32, (rows, 2 * BAND), 1)
    dist = BAND + qi - kj
    has_prev = mb > 0
    valid = (dist >= 0) & (dist <= BAND) & ((kj >= BAND) | has_prev)
    lane = lax.broadcasted_iota(jnp.int32, (BAND, LANES), 1)
    lse_tile = jnp.zeros((BAND, LANES), F32)
    scale = HEAD_DIM ** -0.5
    for g in range(KV_HEADS):
        cs = slice(g * HEAD_DIM, (g + 1) * HEAD_DIM)
        k2 = jnp.concatenate([kp_ref[:, cs], kc_ref[:, cs]], axis=0).astype(BF16)
        v2 = jnp.concatenate([vp_ref[:, cs], vc_ref[:, cs]], axis=0).astype(BF16)
        q4 = jnp.concatenate(
            [q_ref[:, (g * REP + r) * HEAD_DIM:(g * REP + r + 1) * HEAD_DIM] for r in range(REP)], axis=0)
        s = _nt_dot(q4, k2) * scale
        s = jnp.where(valid, s, NEG_INF)
        m = jnp.max(s, axis=-1, keepdims=True)
        p = jnp.exp(s - m)
        den = jnp.sum(p, axis=-1, keepdims=True)
        o = jnp.dot(p.astype(BF16), v2, preferred_element_type=F32) / den
        lse = m + jnp.log(den)
        for r in range(REP):
            h = g * REP + r
            o_ref[:, h * HEAD_DIM:(h + 1) * HEAD_DIM] = o[r * BAND:(r + 1) * BAND].astype(o_ref.dtype)
            lse_tile = jnp.where(lane == h, lse[r * BAND:(r + 1) * BAND], lse_tile)
    lse_ref[...] = lse_tile


def _band_attn(q, k, v, dil, n_batch, seq_len):
    rows = q.shape[0]
    blocks_per_seq = seq_len // dil // BAND

    def cur_map(s, mb):
        return ((s // dil) * blocks_per_seq + mb, s % dil)

    def prev_map(s, mb):
        return ((s // dil) * blocks_per_seq + jnp.maximum(mb - 1, 0), s % dil)

    return pl.pallas_call(
        _band_attn_kernel,
        grid=(n_batch * dil, blocks_per_seq),
        in_specs=[pl.BlockSpec((BAND, ATTN_WIDTH), cur_map),
                  pl.BlockSpec((BAND, KV_WIDTH), prev_map), pl.BlockSpec((BAND, KV_WIDTH), cur_map),
                  pl.BlockSpec((BAND, KV_WIDTH), prev_map), pl.BlockSpec((BAND, KV_WIDTH), cur_map)],
        out_specs=[pl.BlockSpec((BAND, ATTN_WIDTH), cur_map), pl.BlockSpec((BAND, LANES), cur_map)],
        out_shape=[jax.ShapeDtypeStruct((rows, dil * ATTN_WIDTH), BF16),
                   jax.ShapeDtypeStruct((rows, dil * LANES), F32)],
        compiler_params=_params("parallel", "arbitrary"),
        name="band_attn_d%d" % dil,
    )(q, k, k, v, v)


def _post_mix(y2, h_ref, pw_ref, nw_ref, hn_ref, un_ref):
    hn = h_ref[...] + _rms(y2, pw_ref[...])
    hn_ref[...] = hn
    if un_ref is not None:
        un_ref[...] = _rms(hn, nw_ref[...]).astype(un_ref.dtype)


def _attn_out_kernel(*refs, dils, has_next):
    o_refs, l_refs = refs[0:N_GROUPS], refs[N_GROUPS:2 * N_GROUPS]
    gate_ref, w_ref, h_ref, pw_ref = refs[2 * N_GROUPS:2 * N_GROUPS + 4]
    pos = 2 * N_GROUPS + 4
    nw_ref = refs[pos] if has_next else None
    pos += int(has_next)
    hn_ref = refs[pos]
    un_ref = refs[pos + 1] if has_next else None
    pos += 1 + int(has_next)
    y_ref = refs[pos]
    scratch = list(refs[pos + 1:])
    tm = gate_ref.shape[0]
    o_tok, l_tok = [], []
    for g, dil in enumerate(dils):
        if dil == 1:
            o_tok.append(lambda h, ref=o_refs[g]: ref[:, h * HEAD_DIM:(h + 1) * HEAD_DIM].astype(F32))
            l_tok.append(l_refs[g][...])
            continue
        osc, lsc = scratch.pop(0), scratch.pop(0)
        rows = tm // dil
        for r in range(dil):
            for h in range(Q_HEADS):
                c0 = r * ATTN_WIDTH + h * HEAD_DIM
                osc[h, pl.ds(r, rows, stride=dil), :] = o_refs[g][:, c0:c0 + HEAD_DIM].astype(F32)
            lsc[pl.ds(r, rows, stride=dil), :] = l_refs[g][:, r * LANES:(r + 1) * LANES]
        o_tok.append(lambda h, ref=osc: ref[h])
        l_tok.append(lsc[...])
    m = functools.reduce(jnp.maximum, l_tok)
    es = [jnp.exp(l - m) for l in l_tok]
    den = functools.reduce(lambda a, b: a + b, es)
    ws = [e / den for e in es]
    for h in range(Q_HEADS):
        cs = slice(h * HEAD_DIM, (h + 1) * HEAD_DIM)
        y = functools.reduce(lambda a, b: a + b,
                             [ws[g][:, h:h + 1] * o_tok[g](h) for g in range(N_GROUPS)])
        y_ref[:, cs] = (y * _silu(gate_ref[:, cs])).astype(y_ref.dtype)
    y2 = jnp.dot(y_ref[...], w_ref[...], preferred_element_type=F32)
    _post_mix(y2, h_ref, pw_ref, nw_ref, hn_ref, un_ref)


def _attn_out(os_, lses, dils, gate, w, h, post_w, next_w, tm):
    t, d = h.shape
    row = lambda i: (i, 0)
    fixed = lambda i: (0, 0)
    in_specs = [pl.BlockSpec((tm // dl, dl * ATTN_WIDTH), row) for dl in dils]
    in_specs += [pl.BlockSpec((tm // dl, dl * LANES), row) for dl in dils]
    in_specs += [pl.BlockSpec((tm, ATTN_WIDTH), row), pl.BlockSpec((ATTN_WIDTH, d), fixed),
                 pl.BlockSpec((tm, d), row), pl.BlockSpec((1, d), fixed)]
    args = list(os_) + list(lses) + [gate, w, h, post_w.reshape(1, d)]
    out_specs = [pl.BlockSpec((tm, d), row)]
    out_shape = [jax.ShapeDtypeStruct((t, d), F32)]
    if next_w is not None:
        in_specs.append(pl.BlockSpec((1, d), fixed))
        args.append(next_w.reshape(1, d))
        out_specs.append(pl.BlockSpec((tm, d), row))
        out_shape.append(jax.ShapeDtypeStruct((t, d), BF16))
    scratch = [pltpu.VMEM((tm, ATTN_WIDTH), BF16)]
    for dl in dils:
        if dl > 1:
            scratch += [pltpu.VMEM((Q_HEADS, tm, HEAD_DIM), F32), pltpu.VMEM((tm, LANES), F32)]
    res = pl.pallas_call(
        functools.partial(_attn_out_kernel, dils=tuple(dils), has_next=next_w is not None),
        grid=(t // tm,), in_specs=in_specs, out_specs=out_specs, out_shape=out_shape,
        scratch_shapes=scratch, compiler_params=_params("parallel"), name="attn_out")(*args)
    return (res[0], res[1]) if next_w is not None else (res[0], None)


def _out_kernel(*refs, has_next):
    y_ref, w_ref, h_ref, pw_ref = refs[0:4]
    nw_ref = refs[4] if has_next else None
    hn_ref = refs[4 + int(has_next)]
    un_ref = refs[5 + int(has_next)] if has_next else None
    y2 = jnp.dot(y_ref[...], w_ref[...], preferred_element_type=F32)
    _post_mix(y2, h_ref, pw_ref, nw_ref, hn_ref, un_ref)


def _out_proj(y, w, h, post_w, next_w, tm):
    t, k = y.shape
    d = w.shape[1]
    row = lambda i: (i, 0)
    fixed = lambda i: (0, 0)
    in_specs = [pl.BlockSpec((tm, k), row), pl.BlockSpec((k, d), fixed), pl.BlockSpec((tm, d), row),
                pl.BlockSpec((1, d), fixed)]
    args = [y, w, h, post_w.reshape(1, d)]
    out_specs = [pl.BlockSpec((tm, d), row)]
    out_shape = [jax.ShapeDtypeStruct((t, d), F32)]
    if next_w is not None:
        in_specs.append(pl.BlockSpec((1, d), fixed))
        args.append(next_w.reshape(1, d))
        out_specs.append(pl.BlockSpec((tm, d), row))
        out_shape.append(jax.ShapeDtypeStruct((t, d), BF16))
    res = pl.pallas_call(
        functools.partial(_out_kernel, has_next=next_w is not None),
        grid=(t // tm,), in_specs=in_specs, out_specs=out_specs, out_shape=out_shape,
        compiler_params=_params("parallel"), name="out_proj")(*args)
    return (res[0], res[1]) if next_w is not None else (res[0], None)


def _decode_attn_kernel(*refs, n_tok, tiles_per_group):
    q_ref, k_ref, v_ref = refs[0:3]
    n_cache = sum(tiles_per_group)
    c_refs = refs[3:3 + n_cache]
    o_ref, lse_ref = refs[3 + n_cache:5 + n_cache]
    flat_refs = refs[5 + n_cache:]
    for c_ref, flat_ref in zip(c_refs, flat_refs):
        flat_ref[...] = c_ref[...].reshape(BAND * KV_ROWS, HEAD_DIM)
    rows = n_tok * REP
    scale = HEAD_DIM ** -0.5
    row_t = lax.broadcasted_iota(jnp.int32, (rows, BAND), 0) // REP
    col = lax.broadcasted_iota(jnp.int32, (rows, BAND), 1)
    row_t1 = row_t[:, :1]
    lane = lax.broadcasted_iota(jnp.int32, (rows, LANES), 1)
    lse_tile = jnp.zeros((rows, LANES), F32)
    first = 0
    for g in range(N_GROUPS):
        crefs = flat_refs[first:first + tiles_per_group[g]]
        first += tiles_per_group[g]

        def tile(cref, row):
            return cref[pl.ds(row, BAND, stride=KV_ROWS), :].astype(BF16)

        for kh in range(KV_HEADS):
            q16 = q_ref[0, g * KV_HEADS + kh]
            knew = k_ref[0, g][:, kh * HEAD_DIM:(kh + 1) * HEAD_DIM]
            vnew = v_ref[0, g][:, kh * HEAD_DIM:(kh + 1) * HEAD_DIM]
            s_tiles, v_tiles = [], []
            if g == 0:
                v_tiles.append(tile(crefs[0], KV_HEADS + kh))
                s_tiles.append(jnp.where(col >= row_t, _nt_dot(q16, tile(crefs[0], kh)) * scale, NEG_INF))
            else:
                for t in range(n_tok):
                    v_tiles.append(tile(crefs[t], KV_HEADS + kh))
                    s_tiles.append(jnp.where(row_t == t, _nt_dot(q16, tile(crefs[t], kh)) * scale, NEG_INF))
            qf = q16.astype(F32)
            s_new = []
            for t in range(n_tok):
                sn = jnp.sum(qf * knew[t:t + 1, :], axis=-1, keepdims=True) * scale
                ok = (row_t1 >= t) if g == 0 else (row_t1 == t)
                s_new.append(jnp.where(ok, sn, NEG_INF))
            m = s_new[0]
            for sn in s_new[1:]:
                m = jnp.maximum(m, sn)
            for st in s_tiles:
                m = jnp.maximum(m, jnp.max(st, axis=-1, keepdims=True))
            den = jnp.zeros((rows, 1), F32)
            acc = jnp.zeros((rows, HEAD_DIM), F32)
            for st, vt in zip(s_tiles, v_tiles):
                p = jnp.exp(st - m)
                den = den + jnp.sum(p, axis=-1, keepdims=True)
                acc = acc + jnp.dot(p.astype(BF16), vt, preferred_element_type=F32)
            for t in range(n_tok):
                pn = jnp.exp(s_new[t] - m)
                den = den + pn
                acc = acc + pn * vnew[t:t + 1, :]
            o_ref[0, g * KV_HEADS + kh] = (acc / den).astype(o_ref.dtype)
            lse_tile = jnp.where(lane == g * KV_HEADS + kh, m + jnp.log(den), lse_tile)
    lse_ref[0] = lse_tile


def _decode_attn(qs, ks, vs, cache_views, layer, n_seq, n_tok):
    rows = n_tok * REP
    qh = jnp.stack(qs, axis=1).reshape(n_seq, n_tok, N_GROUPS, KV_HEADS, REP, HEAD_DIM)
    qh = jnp.transpose(qh, (0, 2, 3, 1, 4, 5)).reshape(n_seq, N_GROUPS * KV_HEADS, rows, HEAD_DIM)
    k4 = jnp.stack(ks, axis=1).reshape(n_seq, n_tok, N_GROUPS, KV_WIDTH).transpose(0, 2, 1, 3)
    v4 = jnp.stack(vs, axis=1).reshape(n_seq, n_tok, N_GROUPS, KV_WIDTH).transpose(0, 2, 1, 3)
    in_specs = [pl.BlockSpec((1, N_GROUPS * KV_HEADS, rows, HEAD_DIM), lambda b: (b, 0, 0, 0)),
                pl.BlockSpec((1, N_GROUPS, n_tok, KV_WIDTH), lambda b: (b, 0, 0, 0)),
                pl.BlockSpec((1, N_GROUPS, n_tok, KV_WIDTH), lambda b: (b, 0, 0, 0))]
    cargs, tiles_per_group = [], []
    for g, (win, dil) in enumerate(DILATED_GROUPS):
        c = cache_views[g]
        n_layers = c.shape[0]
        assert c.shape[2] == win * KV_ROWS and win // dil == BAND and (dil == 1 or dil >= n_tok)
        cv = c.reshape(n_layers, n_seq, BAND, dil * KV_ROWS, HEAD_DIM)
        tiles_per_group.append(1 if dil == 1 else n_tok)
        for t in range(tiles_per_group[-1]):
            cargs.append(cv)
            in_specs.append(pl.BlockSpec((None, None, BAND, KV_ROWS, HEAD_DIM),
                                         lambda b, t=t: (layer, b, 0, t, 0)))
    o, lse = pl.pallas_call(
        functools.partial(_decode_attn_kernel, n_tok=n_tok, tiles_per_group=tuple(tiles_per_group)),
        grid=(n_seq,),
        in_specs=in_specs,
        out_specs=[pl.BlockSpec((1, N_GROUPS * KV_HEADS, rows, HEAD_DIM), lambda b: (b, 0, 0, 0)),
                   pl.BlockSpec((1, rows, LANES), lambda b: (b, 0, 0))],
        out_shape=[jax.ShapeDtypeStruct((n_seq, N_GROUPS * KV_HEADS, rows, HEAD_DIM), BF16),
                   jax.ShapeDtypeStruct((n_seq, rows, LANES), F32)],
        scratch_shapes=[pltpu.VMEM((BAND * KV_ROWS, HEAD_DIM), F32) for _ in cargs],
        compiler_params=_params("parallel"),
        name="decode_attn",
    )(qh, k4, v4, *cargs)
    o = o.reshape(n_seq, N_GROUPS, KV_HEADS, n_tok, REP, HEAD_DIM)
    o = jnp.transpose(o, (1, 0, 3, 2, 4, 5)).reshape(N_GROUPS, n_seq * n_tok, ATTN_WIDTH)
    lse = lse[:, :, :N_GROUPS * KV_HEADS].reshape(n_seq, n_tok, REP, N_GROUPS, KV_HEADS)
    lse = jnp.transpose(lse, (3, 0, 1, 4, 2)).reshape(N_GROUPS, n_seq * n_tok, Q_HEADS)
    lse = jnp.pad(lse, ((0, 0), (0, 0), (0, LANES - Q_HEADS)))
    return [o[g] for g in range(N_GROUPS)], [lse[g] for g in range(N_GROUPS)]


def _cache_shift_kernel(cur_ref, nxt_ref, new_ref, o_ref):
    nb = cur_ref.shape[1]
    o_ref[0, 0:nb - 1] = cur_ref[0, 1:nb]
    is_last = pl.program_id(1) == pl.num_programs(1) - 1
    o_ref[0, nb - 1] = jnp.where(is_last, new_ref[0, 0], nxt_ref[0, 0])


def _cache_shift(new, cache_view):
    n_layers, n_seq, rows, width = cache_view.shape
    slab = new.shape[2]
    slabs = rows // slab
    nb = min(slabs, CACHE_SHIFT_SLABS)
    c4 = cache_view.reshape(n_layers * n_seq, slabs, slab, width)
    new4 = new.reshape(n_layers * n_seq, 1, slab, width)
    out = pl.pallas_call(
        _cache_shift_kernel,
        grid=(n_layers * n_seq, slabs // nb),
        in_specs=[pl.BlockSpec((1, nb, slab, width), lambda i, j: (i, j, 0, 0)),
                  pl.BlockSpec((1, 1, slab, width), lambda i, j: (i, jnp.minimum((j + 1) * nb, slabs - 1), 0, 0)),
                  pl.BlockSpec((1, 1, slab, width), lambda i, j: (i, 0, 0, 0))],
        out_specs=pl.BlockSpec((1, nb, slab, width), lambda i, j: (i, j, 0, 0)),
        out_shape=jax.ShapeDtypeStruct(c4.shape, c4.dtype),
        compiler_params=_params("parallel", "arbitrary"),
        name="cache_shift",
    )(c4, c4, new4)
    return out.reshape(cache_view.shape)


def _ssd_kernel(xbc_ref, z_ref, dt_ref, h0_ref, tail_ref, cw_ref, cb_ref, dtb_ref, alog_ref, dsk_ref,
                nw_ref, tri_ref, ex_ref, ext_ref, y_ref, hout_ref, state_ref, xp_ref, yacc_ref, *, valid_len):
    c = pl.program_id(1)
    q = SSD_CHUNK
    pad = SUBLANES

    @pl.when(c == 0)
    def _():
        state_ref[...] = h0_ref[0]
        xp_ref[0:pad, :] = tail_ref[0]

    xp_ref[pad:pad + q, :] = xbc_ref[...]
    conv = cb_ref[...]
    for k in range(SSD_CONV):
        off = pad - (SSD_CONV - 1) + k
        conv = conv + cw_ref[k:k + 1, :] * xp_ref[off:off + q, :]
    xp_ref[0:pad, :] = xbc_ref[q - pad:q, :]
    xc = _silu(conv)
    x = xc[:, :SSD_D_INNER]
    bm = xc[:, SSD_D_INNER:SSD_D_INNER + SSD_BC_WIDTH].astype(BF16)
    cm = xc[:, SSD_D_INNER + SSD_BC_WIDTH:].astype(BF16)

    dt = jax.nn.softplus(dt_ref[...] + dtb_ref[...])
    if valid_len < q:
        trow = lax.broadcasted_iota(jnp.int32, (q, LANES), 0)
        dt = jnp.where(trow < valid_len, dt, 0.0)
    a = -jnp.exp(alog_ref[...])
    tri = tri_ref[...]
    acs = _sel_dot_left(tri, dt * a)
    acs_t = acs.T
    dt_t = dt.T
    acs_last = acs[q - 1:q, :]
    ex = ex_ref[...]
    e_in = _sel_dot_right(jnp.exp(acs), ex)
    e_end = _sel_dot_right(jnp.exp(acs_last - acs) * dt, ex)
    xs = x * e_end
    cd_col = jnp.exp(acs_t[:, q - 1:q])
    cd = _sel_dot_left(ext_ref[...], jnp.broadcast_to(cd_col, (LANES, SSD_STATE)))
    tmask = lax.broadcasted_iota(jnp.int32, (q, q), 0) >= lax.broadcasted_iota(jnp.int32, (q, q), 1)
    lane = lax.broadcasted_iota(jnp.int32, (q, LANES), 1)
    for g in range(SSD_GROUPS):
        gs = slice(g * SSD_GROUP_WIDTH, (g + 1) * SSD_GROUP_WIDTH)
        bg = bm[:, g * SSD_STATE:(g + 1) * SSD_STATE]
        cg = cm[:, g * SSD_STATE:(g + 1) * SSD_STATE]
        h_g = state_ref[gs, :]
        y_off = _nt_dot(cg, h_g.astype(BF16)) * e_in[:, gs]
        new_states = jnp.dot(xs[:, gs].T.astype(BF16), bg, preferred_element_type=F32)
        state_ref[gs, :] = cd[gs, :] * h_g + new_states
        cb = _nt_dot(cg, bg)
        heads_per_group = SSD_HEADS // SSD_GROUPS
        for pr in range(heads_per_group // 2):
            e0 = g * heads_per_group + 2 * pr
            wts = []
            for e in (e0, e0 + 1):
                seg = acs[:, e:e + 1] - acs_t[e:e + 1, :]
                decay = jnp.exp(jnp.where(tmask, seg, NEG_INF))
                wts.append((cb * decay * dt_t[e:e + 1, :]).astype(BF16))
            x2 = x[:, e0 * SSD_HEAD_DIM:(e0 + 2) * SSD_HEAD_DIM]
            rhs = jnp.concatenate([jnp.where(lane < SSD_HEAD_DIM, x2, 0.0),
                                   jnp.where(lane >= SSD_HEAD_DIM, x2, 0.0)], axis=0).astype(BF16)
            y_diag = jnp.dot(jnp.concatenate(wts, axis=1), rhs, preferred_element_type=F32)
            ls = slice(e0 * SSD_HEAD_DIM, (e0 + 2) * SSD_HEAD_DIM)
            yacc_ref[:, ls] = y_diag + y_off[:, (2 * pr) * SSD_HEAD_DIM:(2 * pr + 2) * SSD_HEAD_DIM]

    y = (yacc_ref[...] + dsk_ref[...] * x) * _silu(z_ref[...])
    for g in range(SSD_GROUPS):
        gs = slice(g * SSD_GROUP_WIDTH, (g + 1) * SSD_GROUP_WIDTH)
        yg = y[:, gs]
        yg = yg * lax.rsqrt(jnp.mean(yg * yg, axis=-1, keepdims=True) + NORM_EPS)
        y_ref[:, gs] = (yg * nw_ref[:, gs]).astype(y_ref.dtype)

    @pl.when(c == pl.num_programs(1) - 1)
    def _():
        hout_ref[0] = state_ref[...]


def _ssd_core(xbc, z, dt_raw, h0, tail, conv_w, conv_b, dt_bias, a_log, d_skip, norm_w,
              n_seq, n_chunks, layer, valid_len):
    q = SSD_CHUNK
    t = xbc.shape[0]
    pad_h = LANES - SSD_HEADS
    dtb = jnp.pad(dt_bias, (0, pad_h)).reshape(1, LANES)
    alog = jnp.pad(a_log, (0, pad_h)).reshape(1, LANES)
    dsk = jnp.repeat(d_skip, SSD_HEAD_DIM).reshape(1, SSD_D_INNER)
    tri = jnp.tril(jnp.ones((q, q), F32)).astype(BF16)
    tri3 = jnp.concatenate([tri] * 3, axis=1)
    head_of_channel = jnp.arange(SSD_D_INNER) // SSD_HEAD_DIM
    ex = (jnp.arange(LANES)[:, None] == head_of_channel[None, :]).astype(BF16)
    ex3 = jnp.concatenate([ex] * 3, axis=0)
    ext3 = jnp.concatenate([ex.T] * 3, axis=1)
    row = lambda s, c: (s * n_chunks + c, 0)
    fixed = lambda s, c: (0, 0)
    seq3 = lambda s, c: (layer * n_seq + s, 0, 0)
    y, hout = pl.pallas_call(
        functools.partial(_ssd_kernel, valid_len=valid_len),
        grid=(n_seq, n_chunks),
        in_specs=[pl.BlockSpec((q, SSD_CONV_DIM), row), pl.BlockSpec((q, SSD_D_INNER), row),
                  pl.BlockSpec((q, LANES), row),
                  pl.BlockSpec((1, SSD_D_INNER, SSD_STATE), seq3),
                  pl.BlockSpec((1, SUBLANES, SSD_CONV_DIM), seq3),
                  pl.BlockSpec((SSD_CONV, SSD_CONV_DIM), fixed), pl.BlockSpec((1, SSD_CONV_DIM), fixed),
                  pl.BlockSpec((1, LANES), fixed), pl.BlockSpec((1, LANES), fixed),
                  pl.BlockSpec((1, SSD_D_INNER), fixed), pl.BlockSpec((1, SSD_D_INNER), fixed),
                  pl.BlockSpec((q, 3 * q), fixed), pl.BlockSpec((3 * LANES, SSD_D_INNER), fixed),
                  pl.BlockSpec((SSD_D_INNER, 3 * LANES), fixed)],
        out_specs=[pl.BlockSpec((q, SSD_D_INNER), row),
                   pl.BlockSpec((1, SSD_D_INNER, SSD_STATE), lambda s, c: (s, 0, 0))],
        out_shape=[jax.ShapeDtypeStruct((t, SSD_D_INNER), BF16),
                   jax.ShapeDtypeStruct((n_seq, SSD_D_INNER, SSD_STATE), F32)],
        scratch_shapes=[pltpu.VMEM((SSD_D_INNER, SSD_STATE), F32),
                        pltpu.VMEM((q + SUBLANES, SSD_CONV_DIM), F32),
                        pltpu.VMEM((q, SSD_D_INNER), F32)],
        compiler_params=_params("parallel", "arbitrary"),
        name="ssd_core",
    )(xbc, z, dt_raw, h0, tail, conv_w, conv_b.reshape(1, SSD_CONV_DIM), dtb, alog, dsk,
      norm_w.reshape(1, SSD_D_INNER), tri3, ex3, ext3)
    return y, hout


def _rope_tables(pos):
    inv = ROPE_THETA ** (-jnp.arange(HALF_HEAD, dtype=F32) / HALF_HEAD)
    ang = pos.astype(F32)[:, None] * inv[None, :]
    cos, sin = jnp.cos(ang), jnp.sin(ang)
    return jnp.concatenate([cos, cos], axis=1), jnp.concatenate([-sin, sin], axis=1)


def _attn_weights(w_in):
    nq = N_GROUPS * ATTN_WIDTH
    nk = N_GROUPS * KV_WIDTH
    w = w_in.astype(BF16)
    groups = []
    for g in range(N_GROUPS):
        groups.append(jnp.concatenate(
            [w[:, g * ATTN_WIDTH:(g + 1) * ATTN_WIDTH],
             w[:, nq + g * KV_WIDTH:nq + (g + 1) * KV_WIDTH],
             w[:, nq + nk + g * KV_WIDTH:nq + nk + (g + 1) * KV_WIDTH]], axis=1))
    return groups, w[:, nq + 2 * nk:]


def _prompt_kv_buffers(ks, vs, n_seq, seq_len):
    bufs = []
    for g, (win, dil) in enumerate(DILATED_GROUPS):
        keep = min(win, seq_len)
        rows_per_seq = seq_len // dil

        def last(a):
            a = a.reshape(n_seq, rows_per_seq, dil * KV_WIDTH)[:, rows_per_seq - keep // dil:]
            return a.reshape(n_seq, keep, KV_HEADS, HEAD_DIM)

        bufs.append(jnp.stack([last(ks[g]), last(vs[g])], axis=2))
    return bufs


def _ssd_project(u, w_in, tm):
    wz = w_in[:, :SSD_D_INNER].astype(BF16)
    wx = w_in[:, SSD_D_INNER:SSD_D_INNER + SSD_CONV_DIM].astype(BF16)
    wdt = jnp.pad(w_in[:, SSD_D_INNER + SSD_CONV_DIM:], ((0, 0), (0, LANES - SSD_HEADS))).astype(BF16)
    z = _proj(u, wz, F32, tm)
    xbc = _proj(u, wx, F32, tm)
    dt_raw = _proj(u, wdt, F32, tm)
    return z, xbc, dt_raw


def kernel(x_prompt, x_sample, cache_kv_w128, cache_kv_w512, cache_kv_w2048, state_ssm, state_conv,
           norm_pre, norm_post, attn_w_in, attn_w_out, ssd_w_in, ssd_conv_w, ssd_conv_b,
           ssd_dt_bias, ssd_a_log, ssd_d, ssd_norm_w, ssd_w_out):
    n_p, len_p, d = x_prompt.shape
    n_s, len_s, _ = x_sample.shape
    depth = norm_pre.shape[0]
    t_p, t_s = n_p * len_p, n_s * len_s
    hp = x_prompt.reshape(t_p, d)
    hs = x_sample.reshape(t_s, d)
    tm = 256
    tm_p = 512
    up = _rmsnorm(hp, norm_pre[0], tm)
    us = _rmsnorm(hs, norm_pre[0], tm)
    rope_p = _rope_tables(jnp.arange(len_p))
    rope_s = _rope_tables(PAST_LEN + jnp.arange(t_s) % len_s)
    caches = (cache_kv_w128, cache_kv_w512, cache_kv_w2048)
    cache_views = [c.reshape(c.shape[0], c.shape[1], c.shape[2] * KV_ROWS, HEAD_DIM) for c in caches]
    dils = [dil for _, dil in DILATED_GROUPS]

    chunk = SSD_CHUNK
    n_ssd = state_ssm.shape[0]
    h0_p = jnp.zeros((n_p, SSD_D_INNER, SSD_STATE), F32)
    tail_p = jnp.zeros((n_p, SUBLANES, SSD_CONV_DIM), F32)
    h0_s = state_ssm.reshape(n_ssd * n_s, SSD_D_INNER, SSD_STATE)
    tail_s = jnp.pad(state_conv, ((0, 0), (0, 0), (SUBLANES - (SSD_CONV - 1), 0), (0, 0)))
    tail_s = tail_s.reshape(n_ssd * n_s, SUBLANES, SSD_CONV_DIM)

    p_kv = [[], [], []]
    new_kv = [[], [], []]
    p_ssm, p_conv, s_ssm, s_conv = [], [], [], []
    for i in range(depth):
        j = i // 2
        next_w = norm_pre[i + 1] if i + 1 < depth else None
        if i % 2 == 0:
            w_out = attn_w_out[j].astype(BF16)
            w_groups, w_gate = _attn_weights(attn_w_in[j])
            os_, lses, ks, vs = [], [], [], []
            for g, dil in enumerate(dils):
                q, k, v = _qkv_proj(up, w_groups[g], rope_p, dil, tm_p)
                o, lse = _band_attn(q, k, v, dil, n_p, len_p)
                os_.append(o)
                lses.append(lse)
                ks.append(k)
                vs.append(v)
            gate = _proj(up, w_gate, F32, tm_p)
            for g, buf in enumerate(_prompt_kv_buffers(ks, vs, n_p, len_p)):
                p_kv[g].append(buf)
            hp, up = _attn_out(os_, lses, dils, gate, w_out, hp, norm_post[i], next_w, tm)
            qs, ks, vs = [], [], []
            for g in range(N_GROUPS):
                q, k, v = _qkv_proj(us, w_groups[g], rope_s, 1, tm)
                qs.append(q)
                ks.append(k)
                vs.append(v)
            gate = _proj(us, w_gate, F32, tm)
            os_, lses = _decode_attn(qs, ks, vs, cache_views, j, n_s, len_s)
            for g in range(N_GROUPS):
                new_kv[g].append(jnp.stack([ks[g].reshape(n_s, len_s, KV_HEADS, HEAD_DIM),
                                            vs[g].reshape(n_s, len_s, KV_HEADS, HEAD_DIM)], axis=2
                                           ).reshape(n_s, len_s * KV_ROWS, HEAD_DIM))
            hs, us = _attn_out(os_, lses, [1] * N_GROUPS, gate, w_out, hs, norm_post[i], next_w, tm)
        else:
            w_out = ssd_w_out[j].astype(BF16)
            ssd_args = (ssd_conv_w[j], ssd_conv_b[j], ssd_dt_bias[j], ssd_a_log[j], ssd_d[j], ssd_norm_w[j])
            z, xbc, dt_raw = _ssd_project(up, ssd_w_in[j], tm_p)
            y, h_new = _ssd_core(xbc, z, dt_raw, h0_p, tail_p, *ssd_args,
                                 n_seq=n_p, n_chunks=len_p // chunk, layer=0, valid_len=chunk)
            p_ssm.append(h_new.reshape(n_p, SSD_HEADS, SSD_HEAD_DIM, SSD_STATE))
            p_conv.append(xbc.reshape(n_p, len_p, SSD_CONV_DIM)[:, len_p - (SSD_CONV - 1):])
            hp, up = _out_proj(y, w_out, hp, norm_post[i], next_w, tm_p)
            z, xbc, dt_raw = _ssd_project(us, ssd_w_in[j], tm)
            padc = lambda a_: jnp.pad(a_.reshape(n_s, len_s, -1), ((0, 0), (0, chunk - len_s), (0, 0))
                                      ).reshape(n_s * chunk, -1)
            y, h_new = _ssd_core(padc(xbc), padc(z), padc(dt_raw), h0_s, tail_s, *ssd_args,
                                 n_seq=n_s, n_chunks=1, layer=j, valid_len=len_s)
            y = y.reshape(n_s, chunk, SSD_D_INNER)[:, :len_s].reshape(t_s, SSD_D_INNER)
            s_ssm.append(h_new.reshape(n_s, SSD_HEADS, SSD_HEAD_DIM, SSD_STATE))
            xp = jnp.concatenate([state_conv[j], xbc.reshape(n_s, len_s, SSD_CONV_DIM)], axis=1)
            s_conv.append(xp[:, -(SSD_CONV - 1):])
            hs, us = _out_proj(y, w_out, hs, norm_post[i], next_w, tm)

    s_kv = [_cache_shift(jnp.stack(new_kv[g]), cache_views[g]).reshape(caches[g].shape)
            for g in range(N_GROUPS)]
    return (hp.reshape(n_p, len_p, d), hs.reshape(n_s, len_s, d),
            jnp.stack(p_kv[0]), jnp.stack(p_kv[1]), jnp.stack(p_kv[2]),
            jnp.stack(p_ssm), jnp.stack(p_conv),
            s_kv[0], s_kv[1], s_kv[2],
            jnp.stack(s_ssm), jnp.stack(s_conv))
```

```python
import functools

import jax
import jax.numpy as jnp
from jax import lax
from jax.experimental import pallas as pl
from jax.experimental.pallas import tpu as pltpu

F32 = jnp.float32
BF16 = jnp.bfloat16

D_MODEL = 1024
HEAD_DIM = 128
HALF_HEAD = HEAD_DIM // 2
Q_HEADS = 16
KV_HEADS = 4
REP = Q_HEADS // KV_HEADS
ATTN_WIDTH = Q_HEADS * HEAD_DIM
KV_WIDTH = KV_HEADS * HEAD_DIM
QKV_WIDTH = ATTN_WIDTH + 2 * KV_WIDTH
KV_ROWS = 2 * KV_HEADS
DILATED_GROUPS = ((128, 1), (512, 4), (2048, 16))
N_GROUPS = len(DILATED_GROUPS)
BAND = 128
ROPE_THETA = 10000.0
PAST_LEN = 2048
SSD_D_INNER = 2048
SSD_HEAD_DIM = 64
SSD_HEADS = 32
SSD_GROUPS = 4
SSD_GROUP_WIDTH = SSD_D_INNER // SSD_GROUPS
SSD_STATE = 128
SSD_CONV = 4
SSD_CHUNK = 128
SSD_BC_WIDTH = SSD_GROUPS * SSD_STATE
SSD_CONV_DIM = SSD_D_INNER + 2 * SSD_BC_WIDTH
NORM_EPS = 1e-6
LANES = 128
SUBLANES = 8
VMEM_LIMIT_BYTES = 48 * 1024 * 1024
CACHE_SHIFT_SLABS = 128
NEG_INF = float("-inf")


def _params(*semantics):
    return pltpu.CompilerParams(dimension_semantics=semantics, vmem_limit_bytes=VMEM_LIMIT_BYTES)


def _nt_dot(a, b):
    return lax.dot_general(a, b, (((1,), (1,)), ((), ())), preferred_element_type=F32)


def _split3(a):
    hi = a.astype(BF16)
    r1 = a - hi.astype(F32)
    mid = r1.astype(BF16)
    lo = (r1 - mid.astype(F32)).astype(BF16)
    return hi, mid, lo


def _sel_dot_right(a, sel3):
    return jnp.dot(jnp.concatenate(_split3(a), axis=1), sel3, preferred_element_type=F32)


def _sel_dot_left(sel3, a):
    return jnp.dot(sel3, jnp.concatenate(_split3(a), axis=0), preferred_element_type=F32)


def _rms(x, w):
    return x * lax.rsqrt(jnp.mean(x * x, axis=-1, keepdims=True) + NORM_EPS) * w


def _silu(x):
    return x * jax.nn.sigmoid(x)


def _rmsnorm_kernel(x_ref, w_ref, o_ref):
    o_ref[...] = _rms(x_ref[...], w_ref[...]).astype(o_ref.dtype)


def _rmsnorm(x, w, tm):
    t, d = x.shape
    return pl.pallas_call(
        _rmsnorm_kernel,
        grid=(t // tm,),
        in_specs=[pl.BlockSpec((tm, d), lambda i: (i, 0)), pl.BlockSpec((1, d), lambda i: (0, 0))],
        out_specs=pl.BlockSpec((tm, d), lambda i: (i, 0)),
        out_shape=jax.ShapeDtypeStruct((t, d), BF16),
        compiler_params=_params("parallel"),
        name="rmsnorm",
    )(x, w.reshape(1, d))


def _proj_kernel(u_ref, w_ref, o_ref):
    o_ref[...] = jnp.dot(u_ref[...], w_ref[...], preferred_element_type=F32).astype(o_ref.dtype)


def _proj(u, w, out_dtype, tm):
    t, k = u.shape
    n = w.shape[1]
    return pl.pallas_call(
        _proj_kernel,
        grid=(t // tm,),
        in_specs=[pl.BlockSpec((tm, k), lambda i: (i, 0)), pl.BlockSpec((k, n), lambda i: (0, 0))],
        out_specs=pl.BlockSpec((tm, n), lambda i: (i, 0)),
        out_shape=jax.ShapeDtypeStruct((t, n), out_dtype),
        compiler_params=_params("parallel"),
        name="proj",
    )(u, w)


def _qkv_kernel(u_ref, w_ref, cos_ref, sin_ref, q_ref, k_ref, v_ref, sc_ref, *, dil):
    acc = jnp.dot(u_ref[...], w_ref[...], preferred_element_type=F32)
    cos = cos_ref[...]
    sin = sin_ref[...]
    n_rot = (ATTN_WIDTH + KV_WIDTH) // HEAD_DIM
    n_heads = QKV_WIDTH // HEAD_DIM
    rows = acc.shape[0] // dil

    def emit(c, r, val):
        if c < Q_HEADS:
            q_ref[:, r * ATTN_WIDTH + c * HEAD_DIM:r * ATTN_WIDTH + (c + 1) * HEAD_DIM] = val.astype(q_ref.dtype)
        elif c < n_rot:
            c0 = r * KV_WIDTH + (c - Q_HEADS) * HEAD_DIM
            k_ref[:, c0:c0 + HEAD_DIM] = val
        else:
            c0 = r * KV_WIDTH + (c - n_rot) * HEAD_DIM
            v_ref[:, c0:c0 + HEAD_DIM] = val

    for c in range(n_heads):
        x = acc[:, c * HEAD_DIM:(c + 1) * HEAD_DIM]
        if c < n_rot:
            x = x * cos + pltpu.roll(x, HALF_HEAD, 1) * sin
        if dil == 1:
            emit(c, 0, x)
        else:
            sc_ref[c] = x
    if dil > 1:
        for r in range(dil):
            for c in range(n_heads):
                emit(c, r, sc_ref[c, pl.ds(r, rows, stride=dil), :])


def _qkv_proj(u, w, rope, dil, tm):
    t, d = u.shape
    cos, sin = rope
    nblk = cos.shape[0] // tm
    rows = tm // dil
    row = lambda i: (i, 0)
    fixed = lambda i: (0, 0)
    return pl.pallas_call(
        functools.partial(_qkv_kernel, dil=dil),
        grid=(t // tm,),
        in_specs=[pl.BlockSpec((tm, d), row), pl.BlockSpec((d, QKV_WIDTH), fixed),
                  pl.BlockSpec((tm, HEAD_DIM), lambda i: (i % nblk, 0)),
                  pl.BlockSpec((tm, HEAD_DIM), lambda i: (i % nblk, 0))],
        out_specs=[pl.BlockSpec((rows, dil * ATTN_WIDTH), row), pl.BlockSpec((rows, dil * KV_WIDTH), row),
                   pl.BlockSpec((rows, dil * KV_WIDTH), row)],
        out_shape=[jax.ShapeDtypeStruct((t // dil, dil * ATTN_WIDTH), BF16),
                   jax.ShapeDtypeStruct((t // dil, dil * KV_WIDTH), F32),
                   jax.ShapeDtypeStruct((t // dil, dil * KV_WIDTH), F32)],
        scratch_shapes=[pltpu.VMEM((QKV_WIDTH // HEAD_DIM, tm, HEAD_DIM), F32)],
        compiler_params=_params("parallel"),
        name="qkv_proj_d%d" % dil,
    )(u, w, cos, sin)


def _band_attn_kernel(q_ref, kp_ref, kc_ref, vp_ref, vc_ref, o_ref, lse_ref):
    mb = pl.program_id(1)
    rows = REP * BAND
    qi = lax.broadcasted_iota(jnp.int32, (rows, 2 * BAND), 0) % BAND
    kj = lax.broadcasted_iota(jnp.int32, (rows, 2 * BAND), 1)
    dist = BAND + qi - kj
    has_prev = mb > 0
    valid = (dist >= 0) & (dist <= BAND) & ((kj >= BAND) | has_prev)
    lane = lax.broadcasted_iota(jnp.int32, (BAND, LANES), 1)
    lse_tile = jnp.zeros((BAND, LANES), F32)
    scale = HEAD_DIM ** -0.5
    for g in range(KV_HEADS):
        cs = slice(g * HEAD_DIM, (g + 1) * HEAD_DIM)
        k2 = jnp.concatenate([kp_ref[:, cs], kc_ref[:, cs]], axis=0).astype(BF16)
        v2 = jnp.concatenate([vp_ref[:, cs], vc_ref[:, cs]], axis=0).astype(BF16)
        q4 = jnp.concatenate(
            [q_ref[:, (g * REP + r) * HEAD_DIM:(g * REP + r + 1) * HEAD_DIM] for r in range(REP)], axis=0)
        s = _nt_dot(q4, k2) * scale
        s = jnp.where(valid, s, NEG_INF)
        m = jnp.max(s, axis=-1, keepdims=True)
        p = jnp.exp(s - m)
        den = jnp.sum(p, axis=-1, keepdims=True)
        o = jnp.dot(p.astype(BF16), v2, preferred_element_type=F32) / den
        lse = m + jnp.log(den)
        for r in range(REP):
            h = g * REP + r
            o_ref[:, h * HEAD_DIM:(h + 1) * HEAD_DIM] = o[r * BAND:(r + 1) * BAND].astype(o_ref.dtype)
            lse_tile = jnp.where(lane == h, lse[r * BAND:(r + 1) * BAND], lse_tile)
    lse_ref[...] = lse_tile


def _band_attn(q, k, v, dil, n_batch, seq_len):
    rows = q.shape[0]
    blocks_per_seq = seq_len // dil // BAND

    def cur_map(s, mb):
        return ((s // dil) * blocks_per_seq + mb, s % dil)

    def prev_map(s, mb):
        return ((s // dil) * blocks_per_seq + jnp.maximum(mb - 1, 0), s % dil)

    return pl.pallas_call(
        _band_attn_kernel,
        grid=(n_batch * dil, blocks_per_seq),
        in_specs=[pl.BlockSpec((BAND, ATTN_WIDTH), cur_map),
                  pl.BlockSpec((BAND, KV_WIDTH), prev_map), pl.BlockSpec((BAND, KV_WIDTH), cur_map),
                  pl.BlockSpec((BAND, KV_WIDTH), prev_map), pl.BlockSpec((BAND, KV_WIDTH), cur_map)],
        out_specs=[pl.BlockSpec((BAND, ATTN_WIDTH), cur_map), pl.BlockSpec((BAND, LANES), cur_map)],
        out_shape=[jax.ShapeDtypeStruct((rows, dil * ATTN_WIDTH), BF16),
                   jax.ShapeDtypeStruct((rows, dil * LANES), F32)],
        compiler_params=_params("parallel", "arbitrary"),
        name="band_attn_d%d" % dil,
    )(q, k, k, v, v)


def _post_mix(y2, h_ref, pw_ref, nw_ref, hn_ref, un_ref):
    hn = h_ref[...] + _rms(y2, pw_ref[...])
    hn_ref[...] = hn
    if un_ref is not None:
        un_ref[...] = _rms(hn, nw_ref[...]).astype(un_ref.dtype)


def _attn_out_kernel(*refs, dils, has_next):
    o_refs, l_refs = refs[0:N_GROUPS], refs[N_GROUPS:2 * N_GROUPS]
    gate_ref, w_ref, h_ref, pw_ref = refs[2 * N_GROUPS:2 * N_GROUPS + 4]
    pos = 2 * N_GROUPS + 4
    nw_ref = refs[pos] if has_next else None
    pos += int(has_next)
    hn_ref = refs[pos]
    un_ref = refs[pos + 1] if has_next else None
    pos += 1 + int(has_next)
    y_ref = refs[pos]
    scratch = list(refs[pos + 1:])
    tm = gate_ref.shape[0]
    o_tok, l_tok = [], []
    for g, dil in enumerate(dils):
        if dil == 1:
            o_tok.append(lambda h, ref=o_refs[g]: ref[:, h * HEAD_DIM:(h + 1) * HEAD_DIM].astype(F32))
            l_tok.append(l_refs[g][...])
            continue
        osc, lsc = scratch.pop(0), scratch.pop(0)
        rows = tm // dil
        for r in range(dil):
            for h in range(Q_HEADS):
                c0 = r * ATTN_WIDTH + h * HEAD_DIM
                osc[h, pl.ds(r, rows, stride=dil), :] = o_refs[g][:, c0:c0 + HEAD_DIM].astype(F32)
            lsc[pl.ds(r, rows, stride=dil), :] = l_refs[g][:, r * LANES:(r + 1) * LANES]
        o_tok.append(lambda h, ref=osc: ref[h])
        l_tok.append(lsc[...])
    m = functools.reduce(jnp.maximum, l_tok)
    es = [jnp.exp(l - m) for l in l_tok]
    den = functools.reduce(lambda a, b: a + b, es)
    ws = [e / den for e in es]
    for h in range(Q_HEADS):
        cs = slice(h * HEAD_DIM, (h + 1) * HEAD_DIM)
        y = functools.reduce(lambda a, b: a + b,
                             [ws[g][:, h:h + 1] * o_tok[g](h) for g in range(N_GROUPS)])
        y_ref[:, cs] = (y * _silu(gate_ref[:, cs])).astype(y_ref.dtype)
    y2 = jnp.dot(y_ref[...], w_ref[...], preferred_element_type=F32)
    _post_mix(y2, h_ref, pw_ref, nw_ref, hn_ref, un_ref)


def _attn_out(os_, lses, dils, gate, w, h, post_w, next_w, tm):
    t, d = h.shape
    row = lambda i: (i, 0)
    fixed = lambda i: (0, 0)
    in_specs = [pl.BlockSpec((tm // dl, dl * ATTN_WIDTH), row) for dl in dils]
    in_specs += [pl.BlockSpec((tm // dl, dl * LANES), row) for dl in dils]
    in_specs += [pl.BlockSpec((tm, ATTN_WIDTH), row), pl.BlockSpec((ATTN_WIDTH, d), fixed),
                 pl.BlockSpec((tm, d), row), pl.BlockSpec((1, d), fixed)]
    args = list(os_) + list(lses) + [gate, w, h, post_w.reshape(1, d)]
    out_specs = [pl.BlockSpec((tm, d), row)]
    out_shape = [jax.ShapeDtypeStruct((t, d), F32)]
    if next_w is not None:
        in_specs.append(pl.BlockSpec((1, d), fixed))
        args.append(next_w.reshape(1, d))
        out_specs.append(pl.BlockSpec((tm, d), row))
        out_shape.append(jax.ShapeDtypeStruct((t, d), BF16))
    scratch = [pltpu.VMEM((tm, ATTN_WIDTH), BF16)]
    for dl in dils:
        if dl > 1:
            scratch += [pltpu.VMEM((Q_HEADS, tm, HEAD_DIM), F32), pltpu.VMEM((tm, LANES), F32)]
    res = pl.pallas_call(
        functools.partial(_attn_out_kernel, dils=tuple(dils), has_next=next_w is not None),
        grid=(t // tm,), in_specs=in_specs, out_specs=out_specs, out_shape=out_shape,
        scratch_shapes=scratch, compiler_params=_params("parallel"), name="attn_out")(*args)
    return (res[0], res[1]) if next_w is not None else (res[0], None)


def _out_kernel(*refs, has_next):
    y_ref, w_ref, h_ref, pw_ref = refs[0:4]
    nw_ref = refs[4] if has_next else None
    hn_ref = refs[4 + int(has_next)]
    un_ref = refs[5 + int(has_next)] if has_next else None
    y2 = jnp.dot(y_ref[...], w_ref[...], preferred_element_type=F32)
    _post_mix(y2, h_ref, pw_ref, nw_ref, hn_ref, un_ref)


def _out_proj(y, w, h, post_w, next_w, tm):
    t, k = y.shape
    d = w.shape[1]
    row = lambda i: (i, 0)
    fixed = lambda i: (0, 0)
    in_specs = [pl.BlockSpec((tm, k), row), pl.BlockSpec((k, d), fixed), pl.BlockSpec((tm, d), row),
                pl.BlockSpec((1, d), fixed)]
    args = [y, w, h, post_w.reshape(1, d)]
    out_specs = [pl.BlockSpec((tm, d), row)]
    out_shape = [jax.ShapeDtypeStruct((t, d), F32)]
    if next_w is not None:
        in_specs.append(pl.BlockSpec((1, d), fixed))
        args.append(next_w.reshape(1, d))
        out_specs.append(pl.BlockSpec((tm, d), row))
        out_shape.append(jax.ShapeDtypeStruct((t, d), BF16))
    res = pl.pallas_call(
        functools.partial(_out_kernel, has_next=next_w is not None),
        grid=(t // tm,), in_specs=in_specs, out_specs=out_specs, out_shape=out_shape,
        compiler_params=_params("parallel"), name="out_proj")(*args)
    return (res[0], res[1]) if next_w is not None else (res[0], None)


def _decode_attn_kernel(*refs, n_tok, tiles_per_group):
    q_ref, k_ref, v_ref = refs[0:3]
    n_cache = sum(tiles_per_group)
    c_refs = refs[3:3 + n_cache]
    o_ref, lse_ref = refs[3 + n_cache:5 + n_cache]
    flat_refs = refs[5 + n_cache:]
    for c_ref, flat_ref in zip(c_refs, flat_refs):
        flat_ref[...] = c_ref[...].reshape(BAND * KV_ROWS, HEAD_DIM)
    rows = n_tok * REP
    scale = HEAD_DIM ** -0.5
    row_t = lax.broadcasted_iota(jnp.int32, (rows, BAND), 0) // REP
    col = lax.broadcasted_iota(jnp.int32, (rows, BAND), 1)
    row_t1 = row_t[:, :1]
    lane = lax.broadcasted_iota(jnp.int32, (rows, LANES), 1)
    lse_tile = jnp.zeros((rows, LANES), F32)
    first = 0
    for g in range(N_GROUPS):
        crefs = flat_refs[first:first + tiles_per_group[g]]
        first += tiles_per_group[g]

        def tile(cref, row):
            return cref[pl.ds(row, BAND, stride=KV_ROWS), :].astype(BF16)

        for kh in range(KV_HEADS):
            q16 = q_ref[0, g * KV_HEADS + kh]
            knew = k_ref[0, g][:, kh * HEAD_DIM:(kh + 1) * HEAD_DIM]
            vnew = v_ref[0, g][:, kh * HEAD_DIM:(kh + 1) * HEAD_DIM]
            s_tiles, v_tiles = [], []
            if g == 0:
                v_tiles.append(tile(crefs[0], KV_HEADS + kh))
                s_tiles.append(jnp.where(col >= row_t, _nt_dot(q16, tile(crefs[0], kh)) * scale, NEG_INF))
            else:
                for t in range(n_tok):
                    v_tiles.append(tile(crefs[t], KV_HEADS + kh))
                    s_tiles.append(jnp.where(row_t == t, _nt_dot(q16, tile(crefs[t], kh)) * scale, NEG_INF))
            qf = q16.astype(F32)
            s_new = []
            for t in range(n_tok):
                sn = jnp.sum(qf * knew[t:t + 1, :], axis=-1, keepdims=True) * scale
                ok = (row_t1 >= t) if g == 0 else (row_t1 == t)
                s_new.append(jnp.where(ok, sn, NEG_INF))
            m = s_new[0]
            for sn in s_new[1:]:
                m = jnp.maximum(m, sn)
            for st in s_tiles:
                m = jnp.maximum(m, jnp.max(st, axis=-1, keepdims=True))
            den = jnp.zeros((rows, 1), F32)
            acc = jnp.zeros((rows, HEAD_DIM), F32)
            for st, vt in zip(s_tiles, v_tiles):
                p = jnp.exp(st - m)
                den = den + jnp.sum(p, axis=-1, keepdims=True)
                acc = acc + jnp.dot(p.astype(BF16), vt, preferred_element_type=F32)
            for t in range(n_tok):
                pn = jnp.exp(s_new[t] - m)
                den = den + pn
                acc = acc + pn * vnew[t:t + 1, :]
            o_ref[0, g * KV_HEADS + kh] = (acc / den).astype(o_ref.dtype)
            lse_tile = jnp.where(lane == g * KV_HEADS + kh, m + jnp.log(den), lse_tile)
    lse_ref[0] = lse_tile


def _decode_attn(qs, ks, vs, cache_views, layer, n_seq, n_tok):
    rows = n_tok * REP
    qh = jnp.stack(qs, axis=1).reshape(n_seq, n_tok, N_GROUPS, KV_HEADS, REP, HEAD_DIM)
    qh = jnp.transpose(qh, (0, 2, 3, 1, 4, 5)).reshape(n_seq, N_GROUPS * KV_HEADS, rows, HEAD_DIM)
    k4 = jnp.stack(ks, axis=1).reshape(n_seq, n_tok, N_GROUPS, KV_WIDTH).transpose(0, 2, 1, 3)
    v4 = jnp.stack(vs, axis=1).reshape(n_seq, n_tok, N_GROUPS, KV_WIDTH).transpose(0, 2, 1, 3)
    in_specs = [pl.BlockSpec((1, N_GROUPS * KV_HEADS, rows, HEAD_DIM), lambda b: (b, 0, 0, 0)),
                pl.BlockSpec((1, N_GROUPS, n_tok, KV_WIDTH), lambda b: (b, 0, 0, 0)),
                pl.BlockSpec((1, N_GROUPS, n_tok, KV_WIDTH), lambda b: (b, 0, 0, 0))]
    cargs, tiles_per_group = [], []
    for g, (win, dil) in enumerate(DILATED_GROUPS):
        c = cache_views[g]
        n_layers = c.shape[0]
        assert c.shape[2] == win * KV_ROWS and win // dil == BAND and (dil == 1 or dil >= n_tok)
        cv = c.reshape(n_layers, n_seq, BAND, dil * KV_ROWS, HEAD_DIM)
        tiles_per_group.append(1 if dil == 1 else n_tok)
        for t in range(tiles_per_group[-1]):
            cargs.append(cv)
            in_specs.append(pl.BlockSpec((None, None, BAND, KV_ROWS, HEAD_DIM),
                                         lambda b, t=t: (layer, b, 0, t, 0)))
    o, lse = pl.pallas_call(
        functools.partial(_decode_attn_kernel, n_tok=n_tok, tiles_per_group=tuple(tiles_per_group)),
        grid=(n_seq,),
        in_specs=in_specs,
        out_specs=[pl.BlockSpec((1, N_GROUPS * KV_HEADS, rows, HEAD_DIM), lambda b: (b, 0, 0, 0)),
                   pl.BlockSpec((1, rows, LANES), lambda b: (b, 0, 0))],
        out_shape=[jax.ShapeDtypeStruct((n_seq, N_GROUPS * KV_HEADS, rows, HEAD_DIM), BF16),
                   jax.ShapeDtypeStruct((n_seq, rows, LANES), F32)],
        scratch_shapes=[pltpu.VMEM((BAND * KV_ROWS, HEAD_DIM), F32) for _ in cargs],
        compiler_params=_params("parallel"),
        name="decode_attn",
    )(qh, k4, v4, *cargs)
    o = o.reshape(n_seq, N_GROUPS, KV_HEADS, n_tok, REP, HEAD_DIM)
    o = jnp.transpose(o, (1, 0, 3, 2, 4, 5)).reshape(N_GROUPS, n_seq * n_tok, ATTN_WIDTH)
    lse = lse[:, :, :N_GROUPS * KV_HEADS].reshape(n_seq, n_tok, REP, N_GROUPS, KV_HEADS)
    lse = jnp.transpose(lse, (3, 0, 1, 4, 2)).reshape(N_GROUPS, n_seq * n_tok, Q_HEADS)
    lse = jnp.pad(lse, ((0, 0), (0, 0), (0, LANES - Q_HEADS)))
    return [o[g] for g in range(N_GROUPS)], [lse[g] for g in range(N_GROUPS)]


def _cache_shift_kernel(cur_ref, nxt_ref, new_ref, o_ref):
    nb = cur_ref.shape[1]
    o_ref[0, 0:nb - 1] = cur_ref[0, 1:nb]
    is_last = pl.program_id(1) == pl.num_programs(1) - 1
    o_ref[0, nb - 1] = jnp.where(is_last, new_ref[0, 0], nxt_ref[0, 0])


def _cache_shift(new, cache_view):
    n_layers, n_seq, rows, width = cache_view.shape
    slab = new.shape[2]
    slabs = rows // slab
    nb = min(slabs, CACHE_SHIFT_SLABS)
    c4 = cache_view.reshape(n_layers * n_seq, slabs, slab, width)
    new4 = new.reshape(n_layers * n_seq, 1, slab, width)
    out = pl.pallas_call(
        _cache_shift_kernel,
        grid=(n_layers * n_seq, slabs // nb),
        in_specs=[pl.BlockSpec((1, nb, slab, width), lambda i, j: (i, j, 0, 0)),
                  pl.BlockSpec((1, 1, slab, width), lambda i, j: (i, jnp.minimum((j + 1) * nb, slabs - 1), 0, 0)),
                  pl.BlockSpec((1, 1, slab, width), lambda i, j: (i, 0, 0, 0))],
        out_specs=pl.BlockSpec((1, nb, slab, width), lambda i, j: (i, j, 0, 0)),
        out_shape=jax.ShapeDtypeStruct(c4.shape, c4.dtype),
        compiler_params=_params("parallel", "arbitrary"),
        name="cache_shift",
    )(c4, c4, new4)
    return out.reshape(cache_view.shape)


def _pad_rows(a, n):
    if a.shape[0] == n:
        return a
    return jnp.concatenate([a, jnp.zeros((n - a.shape[0],) + a.shape[1:], a.dtype)], axis=0)


def _expand_heads(a, ex3):
    rows = a.shape[0]
    if rows == SSD_CHUNK:
        return _sel_dot_right(a, ex3)
    lane = lax.broadcasted_iota(jnp.int32, (rows, LANES), 1)
    heads_per_tile = LANES // SSD_HEAD_DIM
    assert heads_per_tile == 2
    tiles = []
    for k in range(SSD_D_INNER // LANES):
        lo = jnp.broadcast_to(a[:, 2 * k:2 * k + 1], (rows, LANES))
        hi = jnp.broadcast_to(a[:, 2 * k + 1:2 * k + 2], (rows, LANES))
        tiles.append(jnp.where(lane < SSD_HEAD_DIM, lo, hi))
    return jnp.concatenate(tiles, axis=1)


def _ssd_kernel(xbc_ref, z_ref, dt_ref, h0_ref, tail_ref, cw_ref, cb_ref, dtb_ref, alog_ref, dsk_ref,
                nw_ref, tri_ref, ex_ref, ext_ref, y_ref, hout_ref, state_ref, xp_ref, yacc_ref, *, valid_len):
    c = pl.program_id(1)
    q = SSD_CHUNK
    rows = xbc_ref.shape[0]
    pad = SUBLANES

    @pl.when(c == 0)
    def _():
        state_ref[...] = h0_ref[0]
        xp_ref[0:pad, :] = tail_ref[0]

    xp_ref[pad:pad + rows, :] = xbc_ref[...]
    conv = cb_ref[...]
    for k in range(SSD_CONV):
        off = pad - (SSD_CONV - 1) + k
        conv = conv + cw_ref[k:k + 1, :] * xp_ref[off:off + rows, :]
    xp_ref[0:pad, :] = xbc_ref[rows - pad:rows, :]
    xc = _silu(conv)
    x = xc[:, :SSD_D_INNER]
    bm = _pad_rows(xc[:, SSD_D_INNER:SSD_D_INNER + SSD_BC_WIDTH], q).astype(BF16)
    cm = xc[:, SSD_D_INNER + SSD_BC_WIDTH:].astype(BF16)

    dt = jax.nn.softplus(dt_ref[...] + dtb_ref[...])
    if valid_len < rows:
        trow = lax.broadcasted_iota(jnp.int32, (rows, LANES), 0)
        dt = jnp.where(trow < valid_len, dt, 0.0)
    dt_q = _pad_rows(dt, q)
    a = -jnp.exp(alog_ref[...])
    tri = tri_ref[...]
    acs_q = _sel_dot_left(tri, dt_q * a)
    acs = acs_q[0:rows]
    acs_t = acs_q.T
    dt_t = dt_q.T
    acs_last = acs_q[q - 1:q, :]
    ex = ex_ref[...]
    e_in = _expand_heads(jnp.exp(acs), ex)
    e_end = _expand_heads(jnp.exp(acs_last - acs) * dt, ex)
    xs = x * e_end
    cd_col = jnp.exp(acs_t[:, q - 1:q])
    cd = _sel_dot_left(ext_ref[...], jnp.broadcast_to(cd_col, (LANES, SSD_STATE)))
    tmask = lax.broadcasted_iota(jnp.int32, (rows, q), 0) >= lax.broadcasted_iota(jnp.int32, (rows, q), 1)
    lane = lax.broadcasted_iota(jnp.int32, (q, LANES), 1)
    for g in range(SSD_GROUPS):
        gs = slice(g * SSD_GROUP_WIDTH, (g + 1) * SSD_GROUP_WIDTH)
        bg = bm[:, g * SSD_STATE:(g + 1) * SSD_STATE]
        cg = cm[:, g * SSD_STATE:(g + 1) * SSD_STATE]
        h_g = state_ref[gs, :]
        y_off = _nt_dot(cg, h_g.astype(BF16)) * e_in[:, gs]
        xs_t = _pad_rows(xs[:, gs], q).T.astype(BF16)
        new_states = jnp.dot(xs_t, bg, preferred_element_type=F32)
        state_ref[gs, :] = cd[gs, :] * h_g + new_states
        cb = _nt_dot(cg, bg)
        heads_per_group = SSD_HEADS // SSD_GROUPS
        for pr in range(heads_per_group // 2):
            e0 = g * heads_per_group + 2 * pr
            wts = []
            for e in (e0, e0 + 1):
                seg = acs[:, e:e + 1] - acs_t[e:e + 1, :]
                decay = jnp.exp(jnp.where(tmask, seg, NEG_INF))
                wts.append((cb * decay * dt_t[e:e + 1, :]).astype(BF16))
            x2 = _pad_rows(x[:, e0 * SSD_HEAD_DIM:(e0 + 2) * SSD_HEAD_DIM], q)
            rhs = jnp.concatenate([jnp.where(lane < SSD_HEAD_DIM, x2, 0.0),
                                   jnp.where(lane >= SSD_HEAD_DIM, x2, 0.0)], axis=0).astype(BF16)
            y_diag = jnp.dot(jnp.concatenate(wts, axis=1), rhs, preferred_element_type=F32)
            ls = slice(e0 * SSD_HEAD_DIM, (e0 + 2) * SSD_HEAD_DIM)
            yacc_ref[:, ls] = y_diag + y_off[:, (2 * pr) * SSD_HEAD_DIM:(2 * pr + 2) * SSD_HEAD_DIM]

    y = (yacc_ref[...] + dsk_ref[...] * x) * _silu(z_ref[...])
    for g in range(SSD_GROUPS):
        gs = slice(g * SSD_GROUP_WIDTH, (g + 1) * SSD_GROUP_WIDTH)
        yg = y[:, gs]
        yg = yg * lax.rsqrt(jnp.mean(yg * yg, axis=-1, keepdims=True) + NORM_EPS)
        y_ref[:, gs] = (yg * nw_ref[:, gs]).astype(y_ref.dtype)

    @pl.when(c == pl.num_programs(1) - 1)
    def _():
        hout_ref[0] = state_ref[...]


def _ssd_core(xbc, z, dt_raw, h0, tail, conv_w, conv_b, dt_bias, a_log, d_skip, norm_w,
              n_seq, n_chunks, layer, rows, valid_len):
    q = SSD_CHUNK
    assert rows == q or n_chunks == 1
    t = xbc.shape[0]
    pad_h = LANES - SSD_HEADS
    dtb = jnp.pad(dt_bias, (0, pad_h)).reshape(1, LANES)
    alog = jnp.pad(a_log, (0, pad_h)).reshape(1, LANES)
    dsk = jnp.repeat(d_skip, SSD_HEAD_DIM).reshape(1, SSD_D_INNER)
    tri = jnp.tril(jnp.ones((q, q), F32)).astype(BF16)
    tri3 = jnp.concatenate([tri] * 3, axis=1)
    head_of_channel = jnp.arange(SSD_D_INNER) // SSD_HEAD_DIM
    ex = (jnp.arange(LANES)[:, None] == head_of_channel[None, :]).astype(BF16)
    ex3 = jnp.concatenate([ex] * 3, axis=0)
    ext3 = jnp.concatenate([ex.T] * 3, axis=1)
    row = lambda s, c: (s * n_chunks + c, 0)
    fixed = lambda s, c: (0, 0)
    seq3 = lambda s, c: (layer * n_seq + s, 0, 0)
    y, hout = pl.pallas_call(
        functools.partial(_ssd_kernel, valid_len=valid_len),
        grid=(n_seq, n_chunks),
        in_specs=[pl.BlockSpec((rows, SSD_CONV_DIM), row), pl.BlockSpec((rows, SSD_D_INNER), row),
                  pl.BlockSpec((rows, LANES), row),
                  pl.BlockSpec((1, SSD_D_INNER, SSD_STATE), seq3),
                  pl.BlockSpec((1, SUBLANES, SSD_CONV_DIM), seq3),
                  pl.BlockSpec((SSD_CONV, SSD_CONV_DIM), fixed), pl.BlockSpec((1, SSD_CONV_DIM), fixed),
                  pl.BlockSpec((1, LANES), fixed), pl.BlockSpec((1, LANES), fixed),
                  pl.BlockSpec((1, SSD_D_INNER), fixed), pl.BlockSpec((1, SSD_D_INNER), fixed),
                  pl.BlockSpec((q, 3 * q), fixed), pl.BlockSpec((3 * LANES, SSD_D_INNER), fixed),
                  pl.BlockSpec((SSD_D_INNER, 3 * LANES), fixed)],
        out_specs=[pl.BlockSpec((rows, SSD_D_INNER), row),
                   pl.BlockSpec((1, SSD_D_INNER, SSD_STATE), lambda s, c: (s, 0, 0))],
        out_shape=[jax.ShapeDtypeStruct((t, SSD_D_INNER), BF16),
                   jax.ShapeDtypeStruct((n_seq, SSD_D_INNER, SSD_STATE), F32)],
        scratch_shapes=[pltpu.VMEM((SSD_D_INNER, SSD_STATE), F32),
                        pltpu.VMEM((rows + SUBLANES, SSD_CONV_DIM), F32),
                        pltpu.VMEM((rows, SSD_D_INNER), F32)],
        compiler_params=_params("parallel", "arbitrary"),
        name="ssd_core",
    )(xbc, z, dt_raw, h0, tail, conv_w, conv_b.reshape(1, SSD_CONV_DIM), dtb, alog, dsk,
      norm_w.reshape(1, SSD_D_INNER), tri3, ex3, ext3)
    return y, hout


def _rope_tables(pos):
    inv = ROPE_THETA ** (-jnp.arange(HALF_HEAD, dtype=F32) / HALF_HEAD)
    ang = pos.astype(F32)[:, None] * inv[None, :]
    cos, sin = jnp.cos(ang), jnp.sin(ang)
    return jnp.concatenate([cos, cos], axis=1), jnp.concatenate([-sin, sin], axis=1)


def _attn_weights(w_in):
    nq = N_GROUPS * ATTN_WIDTH
    nk = N_GROUPS * KV_WIDTH
    w = w_in.astype(BF16)
    groups = []
    for g in range(N_GROUPS):
        groups.append(jnp.concatenate(
            [w[:, g * ATTN_WIDTH:(g + 1) * ATTN_WIDTH],
             w[:, nq + g * KV_WIDTH:nq + (g + 1) * KV_WIDTH],
             w[:, nq + nk + g * KV_WIDTH:nq + nk + (g + 1) * KV_WIDTH]], axis=1))
    return groups, w[:, nq + 2 * nk:]


def _prompt_kv_buffers(ks, vs, n_seq, seq_len):
    bufs = []
    for g, (win, dil) in enumerate(DILATED_GROUPS):
        keep = min(win, seq_len)
        rows_per_seq = seq_len // dil

        def last(a):
            a = a.reshape(n_seq, rows_per_seq, dil * KV_WIDTH)[:, rows_per_seq - keep // dil:]
            return a.reshape(n_seq, keep, KV_HEADS, HEAD_DIM)

        bufs.append(jnp.stack([last(ks[g]), last(vs[g])], axis=2))
    return bufs


def _ssd_project(u, w_in, tm):
    wz = w_in[:, :SSD_D_INNER].astype(BF16)
    wx = w_in[:, SSD_D_INNER:SSD_D_INNER + SSD_CONV_DIM].astype(BF16)
    wdt = jnp.pad(w_in[:, SSD_D_INNER + SSD_CONV_DIM:], ((0, 0), (0, LANES - SSD_HEADS))).astype(BF16)
    z = _proj(u, wz, F32, tm)
    xbc = _proj(u, wx, F32, tm)
    dt_raw = _proj(u, wdt, F32, tm)
    return z, xbc, dt_raw


def kernel(x_prompt, x_sample, cache_kv_w128, cache_kv_w512, cache_kv_w2048, state_ssm, state_conv,
           norm_pre, norm_post, attn_w_in, attn_w_out, ssd_w_in, ssd_conv_w, ssd_conv_b,
           ssd_dt_bias, ssd_a_log, ssd_d, ssd_norm_w, ssd_w_out):
    n_p, len_p, d = x_prompt.shape
    n_s, len_s, _ = x_sample.shape
    depth = norm_pre.shape[0]
    t_p, t_s = n_p * len_p, n_s * len_s
    hp = x_prompt.reshape(t_p, d)
    hs = x_sample.reshape(t_s, d)
    tm = 256
    tm_p = 512
    up = _rmsnorm(hp, norm_pre[0], tm)
    us = _rmsnorm(hs, norm_pre[0], tm)
    rope_p = _rope_tables(jnp.arange(len_p))
    rope_s = _rope_tables(PAST_LEN + jnp.arange(t_s) % len_s)
    caches = (cache_kv_w128, cache_kv_w512, cache_kv_w2048)
    cache_views = [c.reshape(c.shape[0], c.shape[1], c.shape[2] * KV_ROWS, HEAD_DIM) for c in caches]
    dils = [dil for _, dil in DILATED_GROUPS]

    chunk = SSD_CHUNK
    rows_s = 2 * SUBLANES
    assert len_s <= rows_s
    n_ssd = state_ssm.shape[0]
    h0_p = jnp.zeros((n_p, SSD_D_INNER, SSD_STATE), F32)
    tail_p = jnp.zeros((n_p, SUBLANES, SSD_CONV_DIM), F32)
    h0_s = state_ssm.reshape(n_ssd * n_s, SSD_D_INNER, SSD_STATE)
    tail_s = jnp.pad(state_conv, ((0, 0), (0, 0), (SUBLANES - (SSD_CONV - 1), 0), (0, 0)))
    tail_s = tail_s.reshape(n_ssd * n_s, SUBLANES, SSD_CONV_DIM)

    p_kv = [[], [], []]
    new_kv = [[], [], []]
    p_ssm, p_conv, s_ssm, s_conv = [], [], [], []
    for i in range(depth):
        j = i // 2
        next_w = norm_pre[i + 1] if i + 1 < depth else None
        if i % 2 == 0:
            w_out = attn_w_out[j].astype(BF16)
            w_groups, w_gate = _attn_weights(attn_w_in[j])
            os_, lses, ks, vs = [], [], [], []
            for g, dil in enumerate(dils):
                q, k, v = _qkv_proj(up, w_groups[g], rope_p, dil, tm_p)
                o, lse = _band_attn(q, k, v, dil, n_p, len_p)
                os_.append(o)
                lses.append(lse)
                ks.append(k)
                vs.append(v)
            gate = _proj(up, w_gate, F32, tm_p)
            for g, buf in enumerate(_prompt_kv_buffers(ks, vs, n_p, len_p)):
                p_kv[g].append(buf)
            hp, up = _attn_out(os_, lses, dils, gate, w_out, hp, norm_post[i], next_w, tm)
            qs, ks, vs = [], [], []
            for g in range(N_GROUPS):
                q, k, v = _qkv_proj(us, w_groups[g], rope_s, 1, tm)
                qs.append(q)
                ks.append(k)
                vs.append(v)
            gate = _proj(us, w_gate, F32, tm)
            os_, lses = _decode_attn(qs, ks, vs, cache_views, j, n_s, len_s)
            for g in range(N_GROUPS):
                new_kv[g].append(jnp.stack([ks[g].reshape(n_s, len_s, KV_HEADS, HEAD_DIM),
                                            vs[g].reshape(n_s, len_s, KV_HEADS, HEAD_DIM)], axis=2
                                           ).reshape(n_s, len_s * KV_ROWS, HEAD_DIM))
            hs, us = _attn_out(os_, lses, [1] * N_GROUPS, gate, w_out, hs, norm_post[i], next_w, tm)
        else:
            w_out = ssd_w_out[j].astype(BF16)
            ssd_args = (ssd_conv_w[j], ssd_conv_b[j], ssd_dt_bias[j], ssd_a_log[j], ssd_d[j], ssd_norm_w[j])
            z, xbc, dt_raw = _ssd_project(up, ssd_w_in[j], tm_p)
            y, h_new = _ssd_core(xbc, z, dt_raw, h0_p, tail_p, *ssd_args,
                                 n_seq=n_p, n_chunks=len_p // chunk, layer=0, rows=chunk, valid_len=chunk)
            p_ssm.append(h_new.reshape(n_p, SSD_HEADS, SSD_HEAD_DIM, SSD_STATE))
            p_conv.append(xbc.reshape(n_p, len_p, SSD_CONV_DIM)[:, len_p - (SSD_CONV - 1):])
            hp, up = _out_proj(y, w_out, hp, norm_post[i], next_w, tm_p)
            z, xbc, dt_raw = _ssd_project(us, ssd_w_in[j], tm)
            padc = lambda a_: jnp.pad(a_.reshape(n_s, len_s, -1), ((0, 0), (0, rows_s - len_s), (0, 0))
                                      ).reshape(n_s * rows_s, -1)
            y, h_new = _ssd_core(padc(xbc), padc(z), padc(dt_raw), h0_s, tail_s, *ssd_args,
                                 n_seq=n_s, n_chunks=1, layer=j, rows=rows_s, valid_len=len_s)
            y = y.reshape(n_s, rows_s, SSD_D_INNER)[:, :len_s].reshape(t_s, SSD_D_INNER)
            s_ssm.append(h_new.reshape(n_s, SSD_HEADS, SSD_HEAD_DIM, SSD_STATE))
            xp = jnp.concatenate([state_conv[j], xbc.reshape(n_s, len_s, SSD_CONV_DIM)], axis=1)
            s_conv.append(xp[:, -(SSD_CONV - 1):])
            hs, us = _out_proj(y, w_out, hs, norm_post[i], next_w, tm)

    s_kv = [_cache_shift(jnp.stack(new_kv[g]), cache_views[g]).reshape(caches[g].shape)
            for g in range(N_GROUPS)]
    return (hp.reshape(n_p, len_p, d), hs.reshape(n_s, len_s, d),
            jnp.stack(p_kv[0]), jnp.stack(p_kv[1]), jnp.stack(p_kv[2]),
            jnp.stack(p_ssm), jnp.stack(p_conv),
            s_kv[0], s_kv[1], s_kv[2],
            jnp.stack(s_ssm), jnp.stack(s_conv))
```

```python
import functools

import jax
import jax.numpy as jnp
from jax import lax
from jax.experimental import pallas as pl
from jax.experimental.pallas import tpu as pltpu

F32 = jnp.float32
BF16 = jnp.bfloat16

D_MODEL = 1024
HEAD_DIM = 128
HALF_HEAD = HEAD_DIM // 2
Q_HEADS = 16
KV_HEADS = 4
REP = Q_HEADS // KV_HEADS
ATTN_WIDTH = Q_HEADS * HEAD_DIM
KV_WIDTH = KV_HEADS * HEAD_DIM
QKV_WIDTH = ATTN_WIDTH + 2 * KV_WIDTH
KV_ROWS = 2 * KV_HEADS
DILATED_GROUPS = ((128, 1), (512, 4), (2048, 16))
N_GROUPS = len(DILATED_GROUPS)
BAND = 128
ROPE_THETA = 10000.0
PAST_LEN = 2048
SSD_D_INNER = 2048
SSD_HEAD_DIM = 64
SSD_HEADS = 32
SSD_GROUPS = 4
SSD_GROUP_WIDTH = SSD_D_INNER // SSD_GROUPS
SSD_STATE = 128
SSD_CONV = 4
SSD_CHUNK = 128
SSD_BC_WIDTH = SSD_GROUPS * SSD_STATE
SSD_CONV_DIM = SSD_D_INNER + 2 * SSD_BC_WIDTH
NORM_EPS = 1e-6
LANES = 128
SUBLANES = 8
VMEM_LIMIT_BYTES = 48 * 1024 * 1024
CACHE_SHIFT_SLABS = 128
NEG_INF = float("-inf")


def _params(*semantics):
    return pltpu.CompilerParams(dimension_semantics=semantics, vmem_limit_bytes=VMEM_LIMIT_BYTES)


def _nt_dot(a, b):
    return lax.dot_general(a, b, (((1,), (1,)), ((), ())), preferred_element_type=F32)


def _split3(a):
    hi = a.astype(BF16)
    r1 = a - hi.astype(F32)
    mid = r1.astype(BF16)
    lo = (r1 - mid.astype(F32)).astype(BF16)
    return hi, mid, lo


def _sel_dot_right(a, sel3):
    return jnp.dot(jnp.concatenate(_split3(a), axis=1), sel3, preferred_element_type=F32)


def _sel_dot_left(sel3, a):
    return jnp.dot(sel3, jnp.concatenate(_split3(a), axis=0), preferred_element_type=F32)


def _rms(x, w):
    return x * lax.rsqrt(jnp.mean(x * x, axis=-1, keepdims=True) + NORM_EPS) * w


def _silu(x):
    return x * jax.nn.sigmoid(x)


def _rmsnorm_kernel(x_ref, w_ref, o_ref):
    o_ref[...] = _rms(x_ref[...], w_ref[...]).astype(o_ref.dtype)


def _rmsnorm(x, w, tm):
    t, d = x.shape
    return pl.pallas_call(
        _rmsnorm_kernel,
        grid=(t // tm,),
        in_specs=[pl.BlockSpec((tm, d), lambda i: (i, 0)), pl.BlockSpec((1, d), lambda i: (0, 0))],
        out_specs=pl.BlockSpec((tm, d), lambda i: (i, 0)),
        out_shape=jax.ShapeDtypeStruct((t, d), BF16),
        compiler_params=_params("parallel"),
        name="rmsnorm",
    )(x, w.reshape(1, d))


def _proj_kernel(u_ref, w_ref, o_ref):
    o_ref[...] = jnp.dot(u_ref[...], w_ref[...], preferred_element_type=F32).astype(o_ref.dtype)


def _proj(u, w, out_dtype, tm):
    t, k = u.shape
    n = w.shape[1]
    return pl.pallas_call(
        _proj_kernel,
        grid=(t // tm,),
        in_specs=[pl.BlockSpec((tm, k), lambda i: (i, 0)), pl.BlockSpec((k, n), lambda i: (0, 0))],
        out_specs=pl.BlockSpec((tm, n), lambda i: (i, 0)),
        out_shape=jax.ShapeDtypeStruct((t, n), out_dtype),
        compiler_params=_params("parallel"),
        name="proj",
    )(u, w)


def _qkv_kernel(u_ref, w_ref, cos_ref, sin_ref, q_ref, k_ref, v_ref, sc_ref, *, dil):
    acc = jnp.dot(u_ref[...], w_ref[...], preferred_element_type=F32)
    cos = cos_ref[...]
    sin = sin_ref[...]
    n_rot = (ATTN_WIDTH + KV_WIDTH) // HEAD_DIM
    n_heads = QKV_WIDTH // HEAD_DIM
    rows = acc.shape[0] // dil

    def emit(c, r, val):
        if c < Q_HEADS:
            q_ref[:, r * ATTN_WIDTH + c * HEAD_DIM:r * ATTN_WIDTH + (c + 1) * HEAD_DIM] = val.astype(q_ref.dtype)
        elif c < n_rot:
            c0 = r * KV_WIDTH + (c - Q_HEADS) * HEAD_DIM
            k_ref[:, c0:c0 + HEAD_DIM] = val
        else:
            c0 = r * KV_WIDTH + (c - n_rot) * HEAD_DIM
            v_ref[:, c0:c0 + HEAD_DIM] = val

    for c in range(n_heads):
        x = acc[:, c * HEAD_DIM:(c + 1) * HEAD_DIM]
        if c < n_rot:
            x = x * cos + pltpu.roll(x, HALF_HEAD, 1) * sin
        if dil == 1:
            emit(c, 0, x)
        else:
            sc_ref[c] = x
    if dil > 1:
        for r in range(dil):
            for c in range(n_heads):
                emit(c, r, sc_ref[c, pl.ds(r, rows, stride=dil), :])


def _qkv_proj(u, w, rope, dil, tm):
    t, d = u.shape
    cos, sin = rope
    nblk = cos.shape[0] // tm
    rows = tm // dil
    row = lambda i: (i, 0)
    fixed = lambda i: (0, 0)
    return pl.pallas_call(
        functools.partial(_qkv_kernel, dil=dil),
        grid=(t // tm,),
        in_specs=[pl.BlockSpec((tm, d), row), pl.BlockSpec((d, QKV_WIDTH), fixed),
                  pl.BlockSpec((tm, HEAD_DIM), lambda i: (i % nblk, 0)),
                  pl.BlockSpec((tm, HEAD_DIM), lambda i: (i % nblk, 0))],
        out_specs=[pl.BlockSpec((rows, dil * ATTN_WIDTH), row), pl.BlockSpec((rows, dil * KV_WIDTH), row),
                   pl.BlockSpec((rows, dil * KV_WIDTH), row)],
        out_shape=[jax.ShapeDtypeStruct((t // dil, dil * ATTN_WIDTH), BF16),
                   jax.ShapeDtypeStruct((t // dil, dil * KV_WIDTH), F32),
                   jax.ShapeDtypeStruct((t // dil, dil * KV_WIDTH), F32)],
        scratch_shapes=[pltpu.VMEM((QKV_WIDTH // HEAD_DIM, tm, HEAD_DIM), F32)],
        compiler_params=_params("parallel"),
        name="qkv_proj_d%d" % dil,
    )(u, w, cos, sin)


def _band_attn_kernel(q_ref, kp_ref, kc_ref, vp_ref, vc_ref, o_ref, lse_ref):
    mb = pl.program_id(1)
    rows = REP * BAND
    qi = lax.broadcasted_iota(jnp.int32, (rows, 2 * BAND), 0) % BAND
    kj = lax.broadcasted_iota(jnp.int32, (rows, 2 * BAND), 1)
    dist = BAND + qi - kj
    has_prev = mb > 0
    valid = (dist >= 0) & (dist <= BAND) & ((kj >= BAND) | has_prev)
    lane = lax.broadcasted_iota(jnp.int32, (BAND, LANES), 1)
    lse_tile = jnp.zeros((BAND, LANES), F32)
    scale = HEAD_DIM ** -0.5
    for g in range(KV_HEADS):
        cs = slice(g * HEAD_DIM, (g + 1) * HEAD_DIM)
        k2 = jnp.concatenate([kp_ref[:, cs], kc_ref[:, cs]], axis=0).astype(BF16)
        v2 = jnp.concatenate([vp_ref[:, cs], vc_ref[:, cs]], axis=0).astype(BF16)
        q4 = jnp.concatenate(
            [q_ref[:, (g * REP + r) * HEAD_DIM:(g * REP + r + 1) * HEAD_DIM] for r in range(REP)], axis=0)
        s = _nt_dot(q4, k2) * scale
        s = jnp.where(valid, s, NEG_INF)
        m = jnp.max(s, axis=-1, keepdims=True)
        p = jnp.exp(s - m)
        den = jnp.sum(p, axis=-1, keepdims=True)
        o = jnp.dot(p.astype(BF16), v2, preferred_element_type=F32) / den
        lse = m + jnp.log(den)
        for r in range(REP):
            h = g * REP + r
            o_ref[:, h * HEAD_DIM:(h + 1) * HEAD_DIM] = o[r * BAND:(r + 1) * BAND].astype(o_ref.dtype)
            lse_tile = jnp.where(lane == h, lse[r * BAND:(r + 1) * BAND], lse_tile)
    lse_ref[...] = lse_tile


def _band_attn(q, k, v, dil, n_batch, seq_len):
    rows = q.shape[0]
    blocks_per_seq = seq_len // dil // BAND

    def cur_map(s, mb):
        return ((s // dil) * blocks_per_seq + mb, s % dil)

    def prev_map(s, mb):
        return ((s // dil) * blocks_per_seq + jnp.maximum(mb - 1, 0), s % dil)

    return pl.pallas_call(
        _band_attn_kernel,
        grid=(n_batch * dil, blocks_per_seq),
        in_specs=[pl.BlockSpec((BAND, ATTN_WIDTH), cur_map),
                  pl.BlockSpec((BAND, KV_WIDTH), prev_map), pl.BlockSpec((BAND, KV_WIDTH), cur_map),
                  pl.BlockSpec((BAND, KV_WIDTH), prev_map), pl.BlockSpec((BAND, KV_WIDTH), cur_map)],
        out_specs=[pl.BlockSpec((BAND, ATTN_WIDTH), cur_map), pl.BlockSpec((BAND, LANES), cur_map)],
        out_shape=[jax.ShapeDtypeStruct((rows, dil * ATTN_WIDTH), BF16),
                   jax.ShapeDtypeStruct((rows, dil * LANES), F32)],
        compiler_params=_params("parallel", "arbitrary"),
        name="band_attn_d%d" % dil,
    )(q, k, k, v, v)


def _post_mix(y2, h_ref, pw_ref, nw_ref, hn_ref, un_ref):
    hn = h_ref[...] + _rms(y2, pw_ref[...])
    hn_ref[...] = hn
    if un_ref is not None:
        un_ref[...] = _rms(hn, nw_ref[...]).astype(un_ref.dtype)


def _attn_out_kernel(*refs, dils, has_next):
    o_refs, l_refs = refs[0:N_GROUPS], refs[N_GROUPS:2 * N_GROUPS]
    gate_ref, w_ref, h_ref, pw_ref = refs[2 * N_GROUPS:2 * N_GROUPS + 4]
    pos = 2 * N_GROUPS + 4
    nw_ref = refs[pos] if has_next else None
    pos += int(has_next)
    hn_ref = refs[pos]
    un_ref = refs[pos + 1] if has_next else None
    pos += 1 + int(has_next)
    y_ref = refs[pos]
    scratch = list(refs[pos + 1:])
    tm = gate_ref.shape[0]
    o_tok, l_tok = [], []
    for g, dil in enumerate(dils):
        if dil == 1:
            o_tok.append(lambda h, ref=o_refs[g]: ref[:, h * HEAD_DIM:(h + 1) * HEAD_DIM].astype(F32))
            l_tok.append(l_refs[g][...])
            continue
        osc, lsc = scratch.pop(0), scratch.pop(0)
        rows = tm // dil
        for r in range(dil):
            for h in range(Q_HEADS):
                c0 = r * ATTN_WIDTH + h * HEAD_DIM
                osc[h, pl.ds(r, rows, stride=dil), :] = o_refs[g][:, c0:c0 + HEAD_DIM].astype(F32)
            lsc[pl.ds(r, rows, stride=dil), :] = l_refs[g][:, r * LANES:(r + 1) * LANES]
        o_tok.append(lambda h, ref=osc: ref[h])
        l_tok.append(lsc[...])
    m = functools.reduce(jnp.maximum, l_tok)
    es = [jnp.exp(l - m) for l in l_tok]
    den = functools.reduce(lambda a, b: a + b, es)
    ws = [e / den for e in es]
    for h in range(Q_HEADS):
        cs = slice(h * HEAD_DIM, (h + 1) * HEAD_DIM)
        y = functools.reduce(lambda a, b: a + b,
                             [ws[g][:, h:h + 1] * o_tok[g](h) for g in range(N_GROUPS)])
        y_ref[:, cs] = (y * _silu(gate_ref[:, cs])).astype(y_ref.dtype)
    y2 = jnp.dot(y_ref[...], w_ref[...], preferred_element_type=F32)
    _post_mix(y2, h_ref, pw_ref, nw_ref, hn_ref, un_ref)


def _attn_out(os_, lses, dils, gate, w, h, post_w, next_w, tm):
    t, d = h.shape
    row = lambda i: (i, 0)
    fixed = lambda i: (0, 0)
    in_specs = [pl.BlockSpec((tm // dl, dl * ATTN_WIDTH), row) for dl in dils]
    in_specs += [pl.BlockSpec((tm // dl, dl * LANES), row) for dl in dils]
    in_specs += [pl.BlockSpec((tm, ATTN_WIDTH), row), pl.BlockSpec((ATTN_WIDTH, d), fixed),
                 pl.BlockSpec((tm, d), row), pl.BlockSpec((1, d), fixed)]
    args = list(os_) + list(lses) + [gate, w, h, post_w.reshape(1, d)]
    out_specs = [pl.BlockSpec((tm, d), row)]
    out_shape = [jax.ShapeDtypeStruct((t, d), F32)]
    if next_w is not None:
        in_specs.append(pl.BlockSpec((1, d), fixed))
        args.append(next_w.reshape(1, d))
        out_specs.append(pl.BlockSpec((tm, d), row))
        out_shape.append(jax.ShapeDtypeStruct((t, d), BF16))
    scratch = [pltpu.VMEM((tm, ATTN_WIDTH), BF16)]
    for dl in dils:
        if dl > 1:
            scratch += [pltpu.VMEM((Q_HEADS, tm, HEAD_DIM), F32), pltpu.VMEM((tm, LANES), F32)]
    res = pl.pallas_call(
        functools.partial(_attn_out_kernel, dils=tuple(dils), has_next=next_w is not None),
        grid=(t // tm,), in_specs=in_specs, out_specs=out_specs, out_shape=out_shape,
        scratch_shapes=scratch, compiler_params=_params("parallel"), name="attn_out")(*args)
    return (res[0], res[1]) if next_w is not None else (res[0], None)


def _out_kernel(*refs, has_next):
    y_ref, w_ref, h_ref, pw_ref = refs[0:4]
    nw_ref = refs[4] if has_next else None
    hn_ref = refs[4 + int(has_next)]
    un_ref = refs[5 + int(has_next)] if has_next else None
    y2 = jnp.dot(y_ref[...], w_ref[...], preferred_element_type=F32)
    _post_mix(y2, h_ref, pw_ref, nw_ref, hn_ref, un_ref)


def _out_proj(y, w, h, post_w, next_w, tm):
    t, k = y.shape
    d = w.shape[1]
    row = lambda i: (i, 0)
    fixed = lambda i: (0, 0)
    in_specs = [pl.BlockSpec((tm, k), row), pl.BlockSpec((k, d), fixed), pl.BlockSpec((tm, d), row),
                pl.BlockSpec((1, d), fixed)]
    args = [y, w, h, post_w.reshape(1, d)]
    out_specs = [pl.BlockSpec((tm, d), row)]
    out_shape = [jax.ShapeDtypeStruct((t, d), F32)]
    if next_w is not None:
        in_specs.append(pl.BlockSpec((1, d), fixed))
        args.append(next_w.reshape(1, d))
        out_specs.append(pl.BlockSpec((tm, d), row))
        out_shape.append(jax.ShapeDtypeStruct((t, d), BF16))
    res = pl.pallas_call(
        functools.partial(_out_kernel, has_next=next_w is not None),
        grid=(t // tm,), in_specs=in_specs, out_specs=out_specs, out_shape=out_shape,
        compiler_params=_params("parallel"), name="out_proj")(*args)
    return (res[0], res[1]) if next_w is not None else (res[0], None)


def _decode_attn_kernel(*refs, n_tok, tiles_per_group):
    q_ref, k_ref, v_ref = refs[0:3]
    n_cache = sum(tiles_per_group)
    c_refs = refs[3:3 + n_cache]
    o_ref, lse_ref = refs[3 + n_cache:5 + n_cache]
    flat_refs = refs[5 + n_cache:]
    for c_ref, flat_ref in zip(c_refs, flat_refs):
        flat_ref[...] = c_ref[...].reshape(BAND * KV_ROWS, HEAD_DIM)
    rows = n_tok * REP
    scale = HEAD_DIM ** -0.5
    row_t = lax.broadcasted_iota(jnp.int32, (rows, BAND), 0) // REP
    col = lax.broadcasted_iota(jnp.int32, (rows, BAND), 1)
    row_t1 = row_t[:, :1]
    lane = lax.broadcasted_iota(jnp.int32, (rows, LANES), 1)
    lse_tile = jnp.zeros((rows, LANES), F32)
    first = 0
    chains = []
    for g in range(N_GROUPS):
        crefs = flat_refs[first:first + tiles_per_group[g]]
        first += tiles_per_group[g]

        def tile(cref, row):
            return cref[pl.ds(row, BAND, stride=KV_ROWS), :].astype(BF16)

        for kh in range(KV_HEADS):
            q16 = q_ref[0, g * KV_HEADS + kh]
            knew = k_ref[0, g][:, kh * HEAD_DIM:(kh + 1) * HEAD_DIM]
            vnew = v_ref[0, g][:, kh * HEAD_DIM:(kh + 1) * HEAD_DIM]
            s_tiles, v_tiles = [], []
            if g == 0:
                v_tiles.append(tile(crefs[0], KV_HEADS + kh))
                s_tiles.append(jnp.where(col >= row_t, _nt_dot(q16, tile(crefs[0], kh)) * scale, NEG_INF))
            else:
                for t in range(n_tok):
                    v_tiles.append(tile(crefs[t], KV_HEADS + kh))
                    s_tiles.append(jnp.where(row_t == t, _nt_dot(q16, tile(crefs[t], kh)) * scale, NEG_INF))
            chains.append((g, kh, q16, knew, vnew, s_tiles, v_tiles))
    for g, kh, q16, knew, vnew, s_tiles, v_tiles in chains:
        qf = q16.astype(F32)
        s_new = []
        for t in range(n_tok):
            sn = jnp.sum(qf * knew[t:t + 1, :], axis=-1, keepdims=True) * scale
            ok = (row_t1 >= t) if g == 0 else (row_t1 == t)
            s_new.append(jnp.where(ok, sn, NEG_INF))
        m = s_new[0]
        for sn in s_new[1:]:
            m = jnp.maximum(m, sn)
        for st in s_tiles:
            m = jnp.maximum(m, jnp.max(st, axis=-1, keepdims=True))
        den = jnp.zeros((rows, 1), F32)
        acc = jnp.zeros((rows, HEAD_DIM), F32)
        for st, vt in zip(s_tiles, v_tiles):
            p = jnp.exp(st - m)
            den = den + jnp.sum(p, axis=-1, keepdims=True)
            acc = acc + jnp.dot(p.astype(BF16), vt, preferred_element_type=F32)
        for t in range(n_tok):
            pn = jnp.exp(s_new[t] - m)
            den = den + pn
            acc = acc + pn * vnew[t:t + 1, :]
        o_ref[0, g * KV_HEADS + kh] = (acc / den).astype(o_ref.dtype)
        lse_tile = jnp.where(lane == g * KV_HEADS + kh, m + jnp.log(den), lse_tile)
    lse_ref[0] = lse_tile


def _decode_attn(qs, ks, vs, cache_views, layer, n_seq, n_tok):
    rows = n_tok * REP
    qh = jnp.stack(qs, axis=1).reshape(n_seq, n_tok, N_GROUPS, KV_HEADS, REP, HEAD_DIM)
    qh = jnp.transpose(qh, (0, 2, 3, 1, 4, 5)).reshape(n_seq, N_GROUPS * KV_HEADS, rows, HEAD_DIM)
    k4 = jnp.stack(ks, axis=1).reshape(n_seq, n_tok, N_GROUPS, KV_WIDTH).transpose(0, 2, 1, 3)
    v4 = jnp.stack(vs, axis=1).reshape(n_seq, n_tok, N_GROUPS, KV_WIDTH).transpose(0, 2, 1, 3)
    in_specs = [pl.BlockSpec((1, N_GROUPS * KV_HEADS, rows, HEAD_DIM), lambda b: (b, 0, 0, 0)),
                pl.BlockSpec((1, N_GROUPS, n_tok, KV_WIDTH), lambda b: (b, 0, 0, 0)),
                pl.BlockSpec((1, N_GROUPS, n_tok, KV_WIDTH), lambda b: (b, 0, 0, 0))]
    cargs, tiles_per_group = [], []
    for g, (win, dil) in enumerate(DILATED_GROUPS):
        c = cache_views[g]
        n_layers = c.shape[0]
        assert c.shape[2] == win * KV_ROWS and win // dil == BAND and (dil == 1 or dil >= n_tok)
        cv = c.reshape(n_layers, n_seq, BAND, dil * KV_ROWS, HEAD_DIM)
        tiles_per_group.append(1 if dil == 1 else n_tok)
        for t in range(tiles_per_group[-1]):
            cargs.append(cv)
            in_specs.append(pl.BlockSpec((None, None, BAND, KV_ROWS, HEAD_DIM),
                                         lambda b, t=t: (layer, b, 0, t, 0)))
    o, lse = pl.pallas_call(
        functools.partial(_decode_attn_kernel, n_tok=n_tok, tiles_per_group=tuple(tiles_per_group)),
        grid=(n_seq,),
        in_specs=in_specs,
        out_specs=[pl.BlockSpec((1, N_GROUPS * KV_HEADS, rows, HEAD_DIM), lambda b: (b, 0, 0, 0)),
                   pl.BlockSpec((1, rows, LANES), lambda b: (b, 0, 0))],
        out_shape=[jax.ShapeDtypeStruct((n_seq, N_GROUPS * KV_HEADS, rows, HEAD_DIM), BF16),
                   jax.ShapeDtypeStruct((n_seq, rows, LANES), F32)],
        scratch_shapes=[pltpu.VMEM((BAND * KV_ROWS, HEAD_DIM), F32) for _ in cargs],
        compiler_params=_params("parallel"),
        name="decode_attn",
    )(qh, k4, v4, *cargs)
    o = o.reshape(n_seq, N_GROUPS, KV_HEADS, n_tok, REP, HEAD_DIM)
    o = jnp.transpose(o, (1, 0, 3, 2, 4, 5)).reshape(N_GROUPS, n_seq * n_tok, ATTN_WIDTH)
    lse = lse[:, :, :N_GROUPS * KV_HEADS].reshape(n_seq, n_tok, REP, N_GROUPS, KV_HEADS)
    lse = jnp.transpose(lse, (3, 0, 1, 4, 2)).reshape(N_GROUPS, n_seq * n_tok, Q_HEADS)
    lse = jnp.pad(lse, ((0, 0), (0, 0), (0, LANES - Q_HEADS)))
    return [o[g] for g in range(N_GROUPS)], [lse[g] for g in range(N_GROUPS)]


def _cache_shift_kernel(cur_ref, nxt_ref, new_ref, o_ref):
    nb = cur_ref.shape[1]
    o_ref[0, 0:nb - 1] = cur_ref[0, 1:nb]
    is_last = pl.program_id(1) == pl.num_programs(1) - 1
    o_ref[0, nb - 1] = jnp.where(is_last, new_ref[0, 0], nxt_ref[0, 0])


def _cache_shift(new, cache_view):
    n_layers, n_seq, rows, width = cache_view.shape
    slab = new.shape[2]
    slabs = rows // slab
    nb = min(slabs, CACHE_SHIFT_SLABS)
    c4 = cache_view.reshape(n_layers * n_seq, slabs, slab, width)
    new4 = new.reshape(n_layers * n_seq, 1, slab, width)
    out = pl.pallas_call(
        _cache_shift_kernel,
        grid=(n_layers * n_seq, slabs // nb),
        in_specs=[pl.BlockSpec((1, nb, slab, width), lambda i, j: (i, j, 0, 0)),
                  pl.BlockSpec((1, 1, slab, width), lambda i, j: (i, jnp.minimum((j + 1) * nb, slabs - 1), 0, 0)),
                  pl.BlockSpec((1, 1, slab, width), lambda i, j: (i, 0, 0, 0))],
        out_specs=pl.BlockSpec((1, nb, slab, width), lambda i, j: (i, j, 0, 0)),
        out_shape=jax.ShapeDtypeStruct(c4.shape, c4.dtype),
        compiler_params=_params("parallel", "arbitrary"),
        name="cache_shift",
    )(c4, c4, new4)
    return out.reshape(cache_view.shape)


def _pad_rows(a, n):
    if a.shape[0] == n:
        return a
    return jnp.concatenate([a, jnp.zeros((n - a.shape[0],) + a.shape[1:], a.dtype)], axis=0)


def _expand_heads(a, ex3):
    rows = a.shape[0]
    if rows == SSD_CHUNK:
        return _sel_dot_right(a, ex3)
    lane = lax.broadcasted_iota(jnp.int32, (rows, LANES), 1)
    heads_per_tile = LANES // SSD_HEAD_DIM
    assert heads_per_tile == 2
    tiles = []
    for k in range(SSD_D_INNER // LANES):
        lo = jnp.broadcast_to(a[:, 2 * k:2 * k + 1], (rows, LANES))
        hi = jnp.broadcast_to(a[:, 2 * k + 1:2 * k + 2], (rows, LANES))
        tiles.append(jnp.where(lane < SSD_HEAD_DIM, lo, hi))
    return jnp.concatenate(tiles, axis=1)


def _ssd_kernel(xbc_ref, z_ref, dt_ref, h0_ref, tail_ref, cw_ref, cb_ref, dtb_ref, alog_ref, dsk_ref,
                nw_ref, tri_ref, ex_ref, ext_ref, y_ref, hout_ref, state_ref, xp_ref, yacc_ref, *, valid_len):
    c = pl.program_id(1)
    q = SSD_CHUNK
    rows = xbc_ref.shape[0]
    pad = SUBLANES

    @pl.when(c == 0)
    def _():
        state_ref[...] = h0_ref[0]
        xp_ref[0:pad, :] = tail_ref[0]

    xp_ref[pad:pad + rows, :] = xbc_ref[...]
    conv = cb_ref[...]
    for k in range(SSD_CONV):
        off = pad - (SSD_CONV - 1) + k
        conv = conv + cw_ref[k:k + 1, :] * xp_ref[off:off + rows, :]
    xp_ref[0:pad, :] = xbc_ref[rows - pad:rows, :]
    xc = _silu(conv)
    x = xc[:, :SSD_D_INNER]
    bm = _pad_rows(xc[:, SSD_D_INNER:SSD_D_INNER + SSD_BC_WIDTH], q).astype(BF16)
    cm = xc[:, SSD_D_INNER + SSD_BC_WIDTH:].astype(BF16)

    dt = jax.nn.softplus(dt_ref[...] + dtb_ref[...])
    if valid_len < rows:
        trow = lax.broadcasted_iota(jnp.int32, (rows, LANES), 0)
        dt = jnp.where(trow < valid_len, dt, 0.0)
    dt_q = _pad_rows(dt, q)
    a = -jnp.exp(alog_ref[...])
    tri = tri_ref[...]
    acs_q = _sel_dot_left(tri, dt_q * a)
    acs = acs_q[0:rows]
    acs_t = acs_q.T
    dt_t = dt_q.T
    acs_last = acs_q[q - 1:q, :]
    ex = ex_ref[...]
    e_in = _expand_heads(jnp.exp(acs), ex)
    e_end = _expand_heads(jnp.exp(acs_last - acs) * dt, ex)
    xs = x * e_end
    cd_col = jnp.exp(acs_t[:, q - 1:q])
    cd = _sel_dot_left(ext_ref[...], jnp.broadcast_to(cd_col, (LANES, SSD_STATE)))
    tmask = lax.broadcasted_iota(jnp.int32, (rows, q), 0) >= lax.broadcasted_iota(jnp.int32, (rows, q), 1)
    lane = lax.broadcasted_iota(jnp.int32, (q, LANES), 1)
    for g in range(SSD_GROUPS):
        gs = slice(g * SSD_GROUP_WIDTH, (g + 1) * SSD_GROUP_WIDTH)
        bg = bm[:, g * SSD_STATE:(g + 1) * SSD_STATE]
        cg = cm[:, g * SSD_STATE:(g + 1) * SSD_STATE]
        h_g = state_ref[gs, :]
        y_off = _nt_dot(cg, h_g.astype(BF16)) * e_in[:, gs]
        xs_t = _pad_rows(xs[:, gs], q).T.astype(BF16)
        new_states = jnp.dot(xs_t, bg, preferred_element_type=F32)
        state_ref[gs, :] = cd[gs, :] * h_g + new_states
        cb = _nt_dot(cg, bg)
        heads_per_group = SSD_HEADS // SSD_GROUPS
        for pr in range(heads_per_group // 2):
            e0 = g * heads_per_group + 2 * pr
            wts = []
            for e in (e0, e0 + 1):
                seg = acs[:, e:e + 1] - acs_t[e:e + 1, :]
                decay = jnp.exp(jnp.where(tmask, seg, NEG_INF))
                wts.append((cb * decay * dt_t[e:e + 1, :]).astype(BF16))
            x2 = _pad_rows(x[:, e0 * SSD_HEAD_DIM:(e0 + 2) * SSD_HEAD_DIM], q)
            rhs = jnp.concatenate([jnp.where(lane < SSD_HEAD_DIM, x2, 0.0),
                                   jnp.where(lane >= SSD_HEAD_DIM, x2, 0.0)], axis=0).astype(BF16)
            y_diag = jnp.dot(jnp.concatenate(wts, axis=1), rhs, preferred_element_type=F32)
            ls = slice(e0 * SSD_HEAD_DIM, (e0 + 2) * SSD_HEAD_DIM)
            yacc_ref[:, ls] = y_diag + y_off[:, (2 * pr) * SSD_HEAD_DIM:(2 * pr + 2) * SSD_HEAD_DIM]

    y = (yacc_ref[...] + dsk_ref[...] * x) * _silu(z_ref[...])
    for g in range(SSD_GROUPS):
        gs = slice(g * SSD_GROUP_WIDTH, (g + 1) * SSD_GROUP_WIDTH)
        yg = y[:, gs]
        yg = yg * lax.rsqrt(jnp.mean(yg * yg, axis=-1, keepdims=True) + NORM_EPS)
        y_ref[:, gs] = (yg * nw_ref[:, gs]).astype(y_ref.dtype)

    @pl.when(c == pl.num_programs(1) - 1)
    def _():
        hout_ref[0] = state_ref[...]


def _ssd_core(xbc, z, dt_raw, h0, tail, conv_w, conv_b, dt_bias, a_log, d_skip, norm_w,
              n_seq, n_chunks, layer, rows, valid_len):
    q = SSD_CHUNK
    assert rows == q or n_chunks == 1
    t = xbc.shape[0]
    pad_h = LANES - SSD_HEADS
    dtb = jnp.pad(dt_bias, (0, pad_h)).reshape(1, LANES)
    alog = jnp.pad(a_log, (0, pad_h)).reshape(1, LANES)
    dsk = jnp.repeat(d_skip, SSD_HEAD_DIM).reshape(1, SSD_D_INNER)
    tri = jnp.tril(jnp.ones((q, q), F32)).astype(BF16)
    tri3 = jnp.concatenate([tri] * 3, axis=1)
    head_of_channel = jnp.arange(SSD_D_INNER) // SSD_HEAD_DIM
    ex = (jnp.arange(LANES)[:, None] == head_of_channel[None, :]).astype(BF16)
    ex3 = jnp.concatenate([ex] * 3, axis=0)
    ext3 = jnp.concatenate([ex.T] * 3, axis=1)
    row = lambda s, c: (s * n_chunks + c, 0)
    fixed = lambda s, c: (0, 0)
    seq3 = lambda s, c: (layer * n_seq + s, 0, 0)
    y, hout = pl.pallas_call(
        functools.partial(_ssd_kernel, valid_len=valid_len),
        grid=(n_seq, n_chunks),
        in_specs=[pl.BlockSpec((rows, SSD_CONV_DIM), row), pl.BlockSpec((rows, SSD_D_INNER), row),
                  pl.BlockSpec((rows, LANES), row),
                  pl.BlockSpec((1, SSD_D_INNER, SSD_STATE), seq3),
                  pl.BlockSpec((1, SUBLANES, SSD_CONV_DIM), seq3),
                  pl.BlockSpec((SSD_CONV, SSD_CONV_DIM), fixed), pl.BlockSpec((1, SSD_CONV_DIM), fixed),
                  pl.BlockSpec((1, LANES), fixed), pl.BlockSpec((1, LANES), fixed),
                  pl.BlockSpec((1, SSD_D_INNER), fixed), pl.BlockSpec((1, SSD_D_INNER), fixed),
                  pl.BlockSpec((q, 3 * q), fixed), pl.BlockSpec((3 * LANES, SSD_D_INNER), fixed),
                  pl.BlockSpec((SSD_D_INNER, 3 * LANES), fixed)],
        out_specs=[pl.BlockSpec((rows, SSD_D_INNER), row),
                   pl.BlockSpec((1, SSD_D_INNER, SSD_STATE), lambda s, c: (s, 0, 0))],
        out_shape=[jax.ShapeDtypeStruct((t, SSD_D_INNER), BF16),
                   jax.ShapeDtypeStruct((n_seq, SSD_D_INNER, SSD_STATE), F32)],
        scratch_shapes=[pltpu.VMEM((SSD_D_INNER, SSD_STATE), F32),
                        pltpu.VMEM((rows + SUBLANES, SSD_CONV_DIM), F32),
                        pltpu.VMEM((rows, SSD_D_INNER), F32)],
        compiler_params=_params("parallel", "arbitrary"),
        name="ssd_core",
    )(xbc, z, dt_raw, h0, tail, conv_w, conv_b.reshape(1, SSD_CONV_DIM), dtb, alog, dsk,
      norm_w.reshape(1, SSD_D_INNER), tri3, ex3, ext3)
    return y, hout


def _rope_tables(pos):
    inv = ROPE_THETA ** (-jnp.arange(HALF_HEAD, dtype=F32) / HALF_HEAD)
    ang = pos.astype(F32)[:, None] * inv[None, :]
    cos, sin = jnp.cos(ang), jnp.sin(ang)
    return jnp.concatenate([cos, cos], axis=1), jnp.concatenate([-sin, sin], axis=1)


def _attn_weights(w_in):
    nq = N_GROUPS * ATTN_WIDTH
    nk = N_GROUPS * KV_WIDTH
    w = w_in.astype(BF16)
    groups = []
    for g in range(N_GROUPS):
        groups.append(jnp.concatenate(
            [w[:, g * ATTN_WIDTH:(g + 1) * ATTN_WIDTH],
             w[:, nq + g * KV_WIDTH:nq + (g + 1) * KV_WIDTH],
             w[:, nq + nk + g * KV_WIDTH:nq + nk + (g + 1) * KV_WIDTH]], axis=1))
    return groups, w[:, nq + 2 * nk:]


def _prompt_kv_buffers(ks, vs, n_seq, seq_len):
    bufs = []
    for g, (win, dil) in enumerate(DILATED_GROUPS):
        keep = min(win, seq_len)
        rows_per_seq = seq_len // dil

        def last(a):
            a = a.reshape(n_seq, rows_per_seq, dil * KV_WIDTH)[:, rows_per_seq - keep // dil:]
            return a.reshape(n_seq, keep, KV_HEADS, HEAD_DIM)

        bufs.append(jnp.stack([last(ks[g]), last(vs[g])], axis=2))
    return bufs


def _ssd_project(u, w_in, tm):
    wz = w_in[:, :SSD_D_INNER].astype(BF16)
    wx = w_in[:, SSD_D_INNER:SSD_D_INNER + SSD_CONV_DIM].astype(BF16)
    wdt = jnp.pad(w_in[:, SSD_D_INNER + SSD_CONV_DIM:], ((0, 0), (0, LANES - SSD_HEADS))).astype(BF16)
    z = _proj(u, wz, F32, tm)
    xbc = _proj(u, wx, F32, tm)
    dt_raw = _proj(u, wdt, F32, tm)
    return z, xbc, dt_raw


def kernel(x_prompt, x_sample, cache_kv_w128, cache_kv_w512, cache_kv_w2048, state_ssm, state_conv,
           norm_pre, norm_post, attn_w_in, attn_w_out, ssd_w_in, ssd_conv_w, ssd_conv_b,
           ssd_dt_bias, ssd_a_log, ssd_d, ssd_norm_w, ssd_w_out):
    n_p, len_p, d = x_prompt.shape
    n_s, len_s, _ = x_sample.shape
    depth = norm_pre.shape[0]
    t_p, t_s = n_p * len_p, n_s * len_s
    hp = x_prompt.reshape(t_p, d)
    hs = x_sample.reshape(t_s, d)
    tm = 256
    tm_p = 512
    up = _rmsnorm(hp, norm_pre[0], tm)
    us = _rmsnorm(hs, norm_pre[0], tm)
    rope_p = _rope_tables(jnp.arange(len_p))
    rope_s = _rope_tables(PAST_LEN + jnp.arange(t_s) % len_s)
    caches = (cache_kv_w128, cache_kv_w512, cache_kv_w2048)
    cache_views = [c.reshape(c.shape[0], c.shape[1], c.shape[2] * KV_ROWS, HEAD_DIM) for c in caches]
    dils = [dil for _, dil in DILATED_GROUPS]

    chunk = SSD_CHUNK
    rows_s = 2 * SUBLANES
    assert len_s <= rows_s
    n_ssd = state_ssm.shape[0]
    h0_p = jnp.zeros((n_p, SSD_D_INNER, SSD_STATE), F32)
    tail_p = jnp.zeros((n_p, SUBLANES, SSD_CONV_DIM), F32)
    h0_s = state_ssm.reshape(n_ssd * n_s, SSD_D_INNER, SSD_STATE)
    tail_s = jnp.pad(state_conv, ((0, 0), (0, 0), (SUBLANES - (SSD_CONV - 1), 0), (0, 0)))
    tail_s = tail_s.reshape(n_ssd * n_s, SUBLANES, SSD_CONV_DIM)

    p_kv = [[], [], []]
    new_kv = [[], [], []]
    p_ssm, p_conv, s_ssm, s_conv = [], [], [], []
    for i in range(depth):
        j = i // 2
        next_w = norm_pre[i + 1] if i + 1 < depth else None
        if i % 2 == 0:
            w_out = attn_w_out[j].astype(BF16)
            w_groups, w_gate = _attn_weights(attn_w_in[j])
            os_, lses, ks, vs = [], [], [], []
            for g, dil in enumerate(dils):
                q, k, v = _qkv_proj(up, w_groups[g], rope_p, dil, tm_p)
                o, lse = _band_attn(q, k, v, dil, n_p, len_p)
                os_.append(o)
                lses.append(lse)
                ks.append(k)
                vs.append(v)
            gate = _proj(up, w_gate, F32, tm_p)
            for g, buf in enumerate(_prompt_kv_buffers(ks, vs, n_p, len_p)):
                p_kv[g].append(buf)
            hp, up = _attn_out(os_, lses, dils, gate, w_out, hp, norm_post[i], next_w, tm)
            qs, ks, vs = [], [], []
            for g in range(N_GROUPS):
                q, k, v = _qkv_proj(us, w_groups[g], rope_s, 1, tm)
                qs.append(q)
                ks.append(k)
                vs.append(v)
            gate = _proj(us, w_gate, F32, tm)
            os_, lses = _decode_attn(qs, ks, vs, cache_views, j, n_s, len_s)
            for g in range(N_GROUPS):
                new_kv[g].append(jnp.stack([ks[g].reshape(n_s, len_s, KV_HEADS, HEAD_DIM),
                                            vs[g].reshape(n_s, len_s, KV_HEADS, HEAD_DIM)], axis=2
                                           ).reshape(n_s, len_s * KV_ROWS, HEAD_DIM))
            hs, us = _attn_out(os_, lses, [1] * N_GROUPS, gate, w_out, hs, norm_post[i], next_w, tm)
        else:
            w_out = ssd_w_out[j].astype(BF16)
            ssd_args = (ssd_conv_w[j], ssd_conv_b[j], ssd_dt_bias[j], ssd_a_log[j], ssd_d[j], ssd_norm_w[j])
            z, xbc, dt_raw = _ssd_project(up, ssd_w_in[j], tm_p)
            y, h_new = _ssd_core(xbc, z, dt_raw, h0_p, tail_p, *ssd_args,
                                 n_seq=n_p, n_chunks=len_p // chunk, layer=0, rows=chunk, valid_len=chunk)
            p_ssm.append(h_new.reshape(n_p, SSD_HEADS, SSD_HEAD_DIM, SSD_STATE))
            p_conv.append(xbc.reshape(n_p, len_p, SSD_CONV_DIM)[:, len_p - (SSD_CONV - 1):])
            hp, up = _out_proj(y, w_out, hp, norm_post[i], next_w, tm_p)
            z, xbc, dt_raw = _ssd_project(us, ssd_w_in[j], tm)
            padc = lambda a_: jnp.pad(a_.reshape(n_s, len_s, -1), ((0, 0), (0, rows_s - len_s), (0, 0))
                                      ).reshape(n_s * rows_s, -1)
            y, h_new = _ssd_core(padc(xbc), padc(z), padc(dt_raw), h0_s, tail_s, *ssd_args,
                                 n_seq=n_s, n_chunks=1, layer=j, rows=rows_s, valid_len=len_s)
            y = y.reshape(n_s, rows_s, SSD_D_INNER)[:, :len_s].reshape(t_s, SSD_D_INNER)
            s_ssm.append(h_new.reshape(n_s, SSD_HEADS, SSD_HEAD_DIM, SSD_STATE))
            xp = jnp.concatenate([state_conv[j], xbc.reshape(n_s, len_s, SSD_CONV_DIM)], axis=1)
            s_conv.append(xp[:, -(SSD_CONV - 1):])
            hs, us = _out_proj(y, w_out, hs, norm_post[i], next_w, tm)

    s_kv = [_cache_shift(jnp.stack(new_kv[g]), cache_views[g]).reshape(caches[g].shape)
            for g in range(N_GROUPS)]
    return (hp.reshape(n_p, len_p, d), hs.reshape(n_s, len_s, d),
            jnp.stack(p_kv[0]), jnp.stack(p_kv[1]), jnp.stack(p_kv[2]),
            jnp.stack(p_ssm), jnp.stack(p_conv),
            s_kv[0], s_kv[1], s_kv[2],
            jnp.stack(s_ssm), jnp.stack(s_conv))
```
